```python
import math
import jax
import jax.numpy as jnp
from jax import lax
import numpy as np

D_MODEL = 1024
BATCH = 4
SEQ = 4096
DEPTH = 1
DEC_BATCH = 128
DEC_SEQ = 1
PAST_LEN = 8192
PAGE_SIZE = 128

EPS = 1e-6
A_HEADS = 8
A_HEAD_DIM = 64
A_WIDTH = A_HEADS * A_HEAD_DIM
A_SCALE = A_HEAD_DIM ** -0.5
MOBA_BLOCK = 256
MOBA_TOPK = 3
MOBA_QB = 32
PAGES_PER_BLOCK = MOBA_BLOCK // PAGE_SIZE
NUM_BUCKETS = 32
MAX_DISTANCE = 128
R_HEADS = 4
R_KEY_DIM = 128
R_VAL_DIM = 128
R_KW = R_HEADS * R_KEY_DIM
R_VW = R_HEADS * R_VAL_DIM
R_CHUNK = 32
P_HEADS = 8
P_NKEYS = 128
P_NEXP = P_NKEYS * P_NKEYS
P_KEY_DIM = 128
P_HALF = P_KEY_DIM // 2
P_TOPK = 16
P_TOKB = 256
IN_SIZES = (A_WIDTH, A_WIDTH, A_WIDTH, R_KW, R_KW, R_VW, R_VW, D_MODEL, D_MODEL)
IN_WIDTH = sum(IN_SIZES)

kernel_name = 'moba_hgrn2_peer_hybrid_step'


def _rms(x, g):
    xf = x.astype(jnp.float32)
    y = xf * lax.rsqrt(jnp.mean(xf * xf, axis=-1, keepdims=True) + EPS)
    return (y * g.astype(jnp.float32)).astype(x.dtype)


def _split_cols(z, sizes):
    out, start = [], 0
    for s in sizes:
        out.append(z[..., start:start + s])
        start += s
    return out


def _t5_bucket(dist):
    max_exact = NUM_BUCKETS // 2
    n = jnp.maximum(dist, 0)
    nf = jnp.maximum(n, 1).astype(jnp.float32)
    large = max_exact + (jnp.log(nf / max_exact) / math.log(MAX_DISTANCE / max_exact)
                         * (NUM_BUCKETS - max_exact)).astype(jnp.int32)
    return jnp.where(n < max_exact, n, jnp.minimum(large, NUM_BUCKETS - 1))


def _to_blocks(rows):
    B, L, H, dh = rows.shape
    nb = -(-L // MOBA_BLOCK)
    rows = jnp.pad(rows, ((0, 0), (0, nb * MOBA_BLOCK - L), (0, 0), (0, 0)))
    return rows.reshape(B, nb, MOBA_BLOCK, H, dh).transpose(0, 3, 1, 2, 4)


def _moba(q, k, v, pos0, rel_bias, cache):
    B, T, H, dh = q.shape
    nb_past = 0
    if cache is not None:
        ck, cv, pt, l = cache
        nb_past = pos0 // MOBA_BLOCK
        p0 = nb_past * PAGES_PER_BLOCK
        tail_pages = pt[:, p0:]
        n_tail = tail_pages.shape[1] * PAGE_SIZE
        k = jnp.concatenate([ck[l, tail_pages].reshape(B, n_tail, H, dh).astype(k.dtype), k], axis=1)
        v = jnp.concatenate([cv[l, tail_pages].reshape(B, n_tail, H, dh).astype(v.dtype), v], axis=1)
    kt, vt = _to_blocks(k), _to_blocks(v)
    nbt = kt.shape[2]
    km = kt.astype(jnp.float32).mean(axis=3)
    if nb_past > 0:
        def seq_means(pages):
            return ck[l, pages].astype(jnp.float32).reshape(nb_past, MOBA_BLOCK, H, dh).mean(axis=1)
        km = jnp.concatenate([lax.map(seq_means, pt[:, :p0]).transpose(0, 2, 1, 3), km], axis=2)
    nb = nb_past + nbt
    kk = min(MOBA_TOPK, nb)
    rb = rel_bias.T
    bi = jnp.arange(B)[:, None, None, None]
    hi = jnp.arange(H)[None, None, :, None]

    def fetch(idx):
        jt = jnp.clip(idx - nb_past, 0, nbt - 1)
        ks, vs = kt[bi, hi, jt], vt[bi, hi, jt]
        if nb_past > 0:
            lp = jnp.minimum(idx, nb_past - 1)[..., None] * PAGES_PER_BLOCK + jnp.arange(PAGES_PER_BLOCK)
            phys = pt[bi[..., None], lp][..., None]
            ri = jnp.arange(PAGE_SIZE)
            hi6 = hi[..., None, None]
            kc = ck[l, phys, ri, hi6].reshape(ks.shape).astype(ks.dtype)
            vc = cv[l, phys, ri, hi6].reshape(vs.shape).astype(vs.dtype)
            from_cache = (idx < nb_past)[..., None, None]
            ks = jnp.where(from_cache, kc, ks)
            vs = jnp.where(from_cache, vc, vs)
        return ks, vs

    def attend(args):
        qc, pc = args
        Q = qc.shape[1]
        own = pc // MOBA_BLOCK
        s = jnp.einsum('bqhd,bhnd->bqhn', qc.astype(jnp.float32), km)
        s = jnp.where((jnp.arange(nb)[None, :] < own[:, None])[None, :, None, :], s, -jnp.inf)
        _, top = lax.top_k(s, kk)
        own_b = jnp.broadcast_to(own[None, :, None, None], (B, Q, H, 1))
        idx = jnp.concatenate([top, own_b], axis=-1)
        ok = jnp.concatenate([top < own_b, jnp.ones_like(own_b, dtype=bool)], axis=-1)
        ks, vs = fetch(idx)
        dist = pc[None, :, None, None, None] - (idx[..., None] * MOBA_BLOCK + jnp.arange(MOBA_BLOCK))
        bias = rb[hi[..., None], _t5_bucket(dist)].astype(jnp.float32)
        logits = jnp.einsum('bqhd,bqhnkd->bqhnk', qc, ks).astype(jnp.float32) * A_SCALE + bias
        logits = jnp.where(ok[..., None] & (dist >= 0), logits, -jnp.inf)
        p = jax.nn.softmax(logits.reshape(B, Q, H, -1), axis=-1).reshape(logits.shape)
        return jnp.einsum('bqhnk,bqhnkd->bqhd', p.astype(vs.dtype), vs)

    qb = min(MOBA_QB, T)
    nq = -(-T // qb)
    qp = jnp.pad(q, ((0, 0), (0, nq * qb - T), (0, 0), (0, 0)))
    pos = pos0 + jnp.minimum(jnp.arange(nq * qb), T - 1)
    out = lax.map(attend, (qp.reshape(B, nq, qb, H, dh).transpose(1, 0, 2, 3, 4), pos.reshape(nq, qb)))
    return out.transpose(1, 0, 2, 3, 4).reshape(B, nq * qb, H, dh)[:, :T]


def _hgrn2(q, k, v, logf, s0):
    B, T, H, dk = q.shape
    C = min(R_CHUNK, T)
    nc = -(-T // C)
    pad = nc * C - T

    def chunks(a):
        a = jnp.pad(a.astype(jnp.float32), ((0, 0), (0, pad), (0, 0), (0, 0)))
        return a.reshape(B, nc, C, H, a.shape[-1]).transpose(1, 0, 2, 3, 4)

    tril = jnp.tril(jnp.ones((C, C), dtype=bool))

    def step(S, inp):
        qc, kc, vc, lc = inp
        b = jnp.cumsum(lc, axis=1)
        qe = qc * jnp.exp(b)
        ke = kc * jnp.exp(-b)
        att = jnp.where(tril, jnp.einsum('bthk,bshk->bhts', qe, ke), 0.0)
        o = jnp.einsum('bthk,bhkv->bthv', qe, S) + jnp.einsum('bhts,bshv->bthv', att, vc)
        bl = b[:, -1]
        S = S * jnp.exp(bl)[..., None] + jnp.einsum('bshk,bshv->bhkv', kc * jnp.exp(bl[:, None] - b), vc)
        return S, o

    S, o = lax.scan(step, s0.astype(jnp.float32), (chunks(q), chunks(k), chunks(v), chunks(logf)))
    o = o.transpose(1, 0, 2, 3, 4).reshape(B, nc * C, H, v.shape[-1])[:, :T]
    return o, S.astype(s0.dtype)


def _peer(h, w_pq, sub_keys, eu, ev):
    B, T, D = h.shape
    xt = h.reshape(B * T, D)
    n = B * T
    tb = min(P_TOKB, n)
    nt = -(-n // tb)
    xp = jnp.pad(xt, ((0, nt * tb - n), (0, 0))).reshape(nt, tb, D)

    def blk(xb):
        q = (xb @ w_pq).reshape(tb, P_HEADS, 2, P_HALF)
        s = jnp.einsum('thcd,hckd->thck', q, sub_keys).astype(jnp.float32)
        sv, si = lax.top_k(s, P_TOPK)
        cand = sv[:, :, 0, :, None] + sv[:, :, 1, None, :]
        cidx = si[:, :, 0, :, None] * P_NKEYS + si[:, :, 1, None, :]
        fv, fi = lax.top_k(cand.reshape(tb, P_HEADS, -1), P_TOPK)
        eidx = jnp.take_along_axis(cidx.reshape(tb, P_HEADS, -1), fi, axis=-1)
        g = jax.nn.softmax(fv, axis=-1)
        a = jax.nn.gelu(jnp.einsum('td,thkd->thk', xb, eu[eidx]).astype(jnp.float32), approximate=False)
        vv = ev[eidx]
        return jnp.einsum('thk,thkd->td', (g * a).astype(vv.dtype), vv)

    out = lax.map(blk, xp).reshape(nt * tb, D)[:n]
    return out.reshape(B, T, D).astype(h.dtype)


def _layer(x, c, pos0, cache, s0, lb, rel_bias, w_ada, b_ada, g_n1, w_in, g_q, g_k,
           w_a_up, w_r_up, g_o, w_out, g_n2, w_pq, sub_keys, eu, ev):
    B, T, _ = x.shape
    mod = jax.nn.silu(c) @ w_ada + b_ada
    sh1, sc1, gt1, sh2, sc2, gt2 = [m[:, None, :] for m in jnp.split(mod, 6, axis=-1)]
    h = _rms(x, g_n1) * (1 + sc1) + sh1
    qa, ka, va, qr, fr, ir, gr, ga, gb = _split_cols(h @ w_in, IN_SIZES)
    qa = _rms(qa.reshape(B, T, A_HEADS, A_HEAD_DIM), g_q)
    ka = _rms(ka.reshape(B, T, A_HEADS, A_HEAD_DIM), g_k)
    va = va.reshape(B, T, A_HEADS, A_HEAD_DIM)
    o_a = _moba(qa, ka, va, pos0, rel_bias, cache).reshape(B, T, A_WIDTH)
    f = lb + (1.0 - lb) * jax.nn.sigmoid(fr.astype(jnp.float32))
    heads = lambda a: a.reshape(B, T, R_HEADS, -1)
    o_r, s_new = _hgrn2(heads(jax.nn.silu(qr.astype(jnp.float32))), heads(1.0 - f),
                        heads(ir.astype(jnp.float32)), heads(jnp.log(f)), s0)
    o_r = (_rms(o_r, g_o).reshape(B, T, R_VW) * jax.nn.silu(gr.astype(jnp.float32))).astype(x.dtype)
    merged = jax.nn.sigmoid(ga) * (o_a @ w_a_up) + jax.nn.sigmoid(gb) * (o_r @ w_r_up)
    x = x + gt1 * (merged @ w_out)
    h2 = _rms(x, g_n2) * (1 + sc2) + sh2
    x = x + gt2 * _peer(h2, w_pq, sub_keys, eu, ev)
    return x, ka, va, s_new


def setup_inputs(seed: int = 0) -> dict:
    key = jax.random.key(seed)
    ks = jax.random.split(key, 32)
    n_pages = PAST_LEN // PAGE_SIZE
    n_phys = (DEC_BATCH * n_pages * 5) // 4
    nrm = lambda k, shape, s: jax.random.normal(k, shape, jnp.float32) * s
    gain = lambda k, shape: 1.0 + nrm(k, shape, 0.05)
    page_table = jax.random.permutation(ks[7], n_phys)[:DEC_BATCH * n_pages]
    return {
        'x_prompt': nrm(ks[0], (BATCH, SEQ, D_MODEL), 1.0),
        'x_sample': nrm(ks[1], (DEC_BATCH, DEC_SEQ, D_MODEL), 1.0),
        'c_prompt': nrm(ks[2], (BATCH, D_MODEL), 1.0),
        'c_sample': nrm(ks[3], (DEC_BATCH, D_MODEL), 1.0),
        'cache_k': nrm(ks[4], (DEPTH, n_phys, PAGE_SIZE, A_HEADS, A_HEAD_DIM), 1.0),
        'cache_v': nrm(ks[5], (DEPTH, n_phys, PAGE_SIZE, A_HEADS, A_HEAD_DIM), 1.0),
        'state_hgrn': nrm(ks[6], (DEPTH, DEC_BATCH, R_HEADS, R_KEY_DIM, R_VAL_DIM), 0.3),
        'page_table': page_table.reshape(DEC_BATCH, n_pages).astype(jnp.int32),
        'rel_bias': nrm(ks[8], (NUM_BUCKETS, A_HEADS), 0.5),
        'lb_param': nrm(ks[9], (DEPTH + 1, R_KW), 0.1),
        'w_ada': nrm(ks[10], (DEPTH, D_MODEL, 6 * D_MODEL), 0.5 * D_MODEL ** -0.5),
        'b_ada': nrm(ks[11], (DEPTH, 6 * D_MODEL), 0.02),
        'g_norm1': gain(ks[12], (DEPTH, D_MODEL)),
        'w_in': nrm(ks[13], (DEPTH, D_MODEL, IN_WIDTH), D_MODEL ** -0.5),
        'g_qnorm': gain(ks[14], (DEPTH, A_HEAD_DIM)),
        'g_knorm': gain(ks[15], (DEPTH, A_HEAD_DIM)),
        'w_a_up': nrm(ks[16], (DEPTH, A_WIDTH, D_MODEL), A_WIDTH ** -0.5),
        'w_r_up': nrm(ks[17], (DEPTH, R_VW, D_MODEL), R_VW ** -0.5),
        'g_onorm': gain(ks[18], (DEPTH, R_VAL_DIM)),
        'w_out': nrm(ks[19], (DEPTH, D_MODEL, D_MODEL), D_MODEL ** -0.5),
        'g_norm2': gain(ks[20], (DEPTH, D_MODEL)),
        'w_pq': nrm(ks[21], (DEPTH, D_MODEL, P_HEADS * P_KEY_DIM), D_MODEL ** -0.5),
        'sub_keys': nrm(ks[22], (DEPTH, P_HEADS, 2, P_NKEYS, P_HALF), P_HALF ** -0.5),
        'expert_u': nrm(ks[23], (DEPTH, P_NEXP, D_MODEL), D_MODEL ** -0.5),
        'expert_v': nrm(ks[24], (DEPTH, P_NEXP, D_MODEL), 0.3),
    }


def reference(x_prompt, x_sample, c_prompt, c_sample, cache_k, cache_v, state_hgrn, page_table,
              rel_bias, lb_param, w_ada, b_ada, g_norm1, w_in, g_qnorm, g_knorm, w_a_up, w_r_up,
              g_onorm, w_out, g_norm2, w_pq, sub_keys, expert_u, expert_v):
    past_len = page_table.shape[1] * PAGE_SIZE
    lbs = jnp.cumsum(jax.nn.softmax(lb_param.astype(jnp.float32), axis=0), axis=0)
    y_p, y_s = x_prompt, x_sample
    kp, vp, sp, ksl, vsl, ssl = [], [], [], [], [], []
    for l in range(DEPTH):
        w = (w_ada[l], b_ada[l], g_norm1[l], w_in[l], g_qnorm[l], g_knorm[l], w_a_up[l], w_r_up[l],
             g_onorm[l], w_out[l], g_norm2[l], w_pq[l], sub_keys[l], expert_u[l], expert_v[l])
        s0 = jnp.zeros((x_prompt.shape[0], R_HEADS, R_KEY_DIM, R_VAL_DIM), state_hgrn.dtype)
        y_p, k1, v1, s1 = _layer(y_p, c_prompt, 0, None, s0, lbs[l], rel_bias, *w)
        y_s, k2, v2, s2 = _layer(y_s, c_sample, past_len, (cache_k, cache_v, page_table, l),
                                 state_hgrn[l], lbs[l], rel_bias, *w)
        kp.append(k1); vp.append(v1); sp.append(s1)
        ksl.append(k2); vsl.append(v2); ssl.append(s2)
    y_prompt, y_sample = y_p, y_s
    k_prompt, v_prompt, s_prompt = jnp.stack(kp), jnp.stack(vp), jnp.stack(sp)
    k_sample, v_sample, s_sample = jnp.stack(ksl), jnp.stack(vsl), jnp.stack(ssl)
    return (y_prompt, y_sample, k_prompt, v_prompt, s_prompt, k_sample, v_sample, s_sample)
```

```python
import functools
import math

import numpy as np
import jax
import jax.numpy as jnp
from jax import lax
from jax.experimental import pallas as pl
from jax.experimental.pallas import tpu as pltpu

F32 = jnp.float32
BF16 = jnp.bfloat16
HIGHEST = lax.Precision.HIGHEST

D_MODEL = 1024
EPS = 1e-6
PAGE_SIZE = 128
A_HEADS = 8
A_HEAD_DIM = 64
A_WIDTH = A_HEADS * A_HEAD_DIM
A_SCALE = A_HEAD_DIM ** -0.5
MOBA_BLOCK = 256
MOBA_TOPK = 3
NUM_BUCKETS = 32
MAX_DISTANCE = 128
R_HEADS = 4
R_KEY_DIM = 128
R_VAL_DIM = 128
R_WIDTH = R_HEADS * R_KEY_DIM
R_CHUNK = 32
P_HEADS = 8
P_NKEYS = 128
P_KEY_DIM = 128
P_HALF = P_KEY_DIM // 2
P_TOPK = 16
IN_WIDTH = 3 * A_WIDTH + 4 * R_WIDTH + 2 * D_MODEL
COL_BLOCK = 512
N_COL_BLOCKS = IN_WIDTH // COL_BLOCK
NEG = -1e30
VMEM_LIMIT = 56 * 1024 * 1024


def _dot(a, b, precision=None):
    return jnp.dot(a, b, preferred_element_type=F32, precision=precision)


def _dot_nt(a, b, precision=None):
    return lax.dot_general(a, b, (((1,), (1,)), ((), ())), preferred_element_type=F32, precision=precision)


def _silu(x):
    return x * jax.nn.sigmoid(x)


def _params(sem, vmem=VMEM_LIMIT):
    return pltpu.CompilerParams(dimension_semantics=sem, vmem_limit_bytes=vmem)


def _ada_kernel(c_ref, w_ref, b_ref, o_ref):
    o_ref[...] = _dot(_silu(c_ref[...]), w_ref[...], HIGHEST) + b_ref[...]


def _ada(c, w_ada, b_ada):
    n, d = c.shape
    width = w_ada.shape[1]
    tn = 768
    return pl.pallas_call(
        _ada_kernel,
        grid=(width // tn,),
        in_specs=[pl.BlockSpec((n, d), lambda j: (0, 0)),
                  pl.BlockSpec((d, tn), lambda j: (0, j)),
                  pl.BlockSpec((1, tn), lambda j: (0, j))],
        out_specs=pl.BlockSpec((n, tn), lambda j: (0, j)),
        out_shape=jax.ShapeDtypeStruct((n, width), F32),
        compiler_params=_params(("arbitrary",)),
        name="ada_mod",
    )(c, w_ada, b_ada.reshape(1, width))


def _bucket_starts():
    max_exact = NUM_BUCKETS // 2
    n = np.arange(0, 4 * MAX_DISTANCE)
    nf = np.maximum(n, 1).astype(np.float32)
    large = max_exact + (np.log(nf / np.float32(max_exact)) / np.float32(math.log(MAX_DISTANCE / max_exact))
                         * np.float32(NUM_BUCKETS - max_exact)).astype(np.int32)
    bucket = np.where(n < max_exact, n, np.minimum(large, NUM_BUCKETS - 1))
    assert (np.diff(bucket) >= 0).all() and bucket[-1] == NUM_BUCKETS - 1
    return [int(np.argmax(bucket >= b)) for b in range(NUM_BUCKETS)]


_BUCKET_START = _bucket_starts()


def _bias_of_distance(dist, rb_ref, h):
    val = jnp.full(dist.shape, rb_ref[NUM_BUCKETS - 1, h], F32)
    for b in range(NUM_BUCKETS - 2, -1, -1):
        val = jnp.where(dist < _BUCKET_START[b + 1], rb_ref[b, h], val)
    return val


def _bias_kernel(rb_ref, o_ref):
    h = pl.program_id(0)
    d = pl.program_id(1)
    k = lax.broadcasted_iota(jnp.int32, (MOBA_BLOCK, MOBA_BLOCK), 0)
    q = lax.broadcasted_iota(jnp.int32, (MOBA_BLOCK, MOBA_BLOCK), 1)
    dist = d * MOBA_BLOCK + q - k
    val = _bias_of_distance(dist, rb_ref, h)
    o_ref[...] = jnp.where(dist >= 0, val, NEG)


def _bias_tiles(rel_bias):
    return pl.pallas_call(
        _bias_kernel,
        grid=(A_HEADS, 3),
        in_specs=[pl.BlockSpec(memory_space=pltpu.SMEM)],
        out_specs=pl.BlockSpec((None, None, MOBA_BLOCK, MOBA_BLOCK), lambda h, d: (h, d, 0, 0)),
        out_shape=jax.ShapeDtypeStruct((A_HEADS, 3, MOBA_BLOCK, MOBA_BLOCK), F32),
        compiler_params=_params(("arbitrary", "arbitrary")),
        name="moba_bias_tiles",
    )(rel_bias)


def _rms_mod(x, g, sc, sh):
    ms = jnp.mean(x * x, axis=-1, keepdims=True)
    return (x * lax.rsqrt(ms + EPS) * g) * (1.0 + sc) + sh


def _split_bf16(x):
    hi = x.astype(BF16)
    return hi, (x - hi.astype(F32)).astype(BF16)


def _inproj_kernel(x_ref, sc_ref, sh_ref, g_ref, whi_ref, wlo_ref, gqk_ref, bd_ref, o_ref, hhi_ref, hlo_ref):
    j = pl.program_id(1)

    @pl.when(j == 0)
    def _():
        hhi, hlo = _split_bf16(_rms_mod(x_ref[...], g_ref[...], sc_ref[...], sh_ref[...]))
        hhi_ref[...] = hhi
        hlo_ref[...] = hlo

    @pl.when(j < 2)
    def _():
        hhi = hhi_ref[...]
        whi = whi_ref[...]
        z = _dot(hhi, whi) + (_dot(hhi, wlo_ref[...]) + _dot(hlo_ref[...], whi))
        zhi, zlo = _split_bf16(z * z)
        ssq = _dot(zhi, bd_ref[...]) + _dot(zlo, bd_ref[...])
        o_ref[...] = z * lax.rsqrt(ssq * (1.0 / A_HEAD_DIM) + EPS) * gqk_ref[...]

    @pl.when(j >= 2)
    def _():
        o_ref[...] = _dot(hhi_ref[...], whi_ref[...])


def _inproj(x2d, sc, sh, g, whi, wlo, gqk, bd, tm):
    n, d = x2d.shape
    groups, r, _ = sc.shape
    tiles_per_group = n // tm // groups
    mod_spec = pl.BlockSpec((None, r, d), lambda i, j: (i // tiles_per_group, 0, 0))
    return pl.pallas_call(
        _inproj_kernel,
        grid=(n // tm, N_COL_BLOCKS),
        in_specs=[pl.BlockSpec((tm, d), lambda i, j: (i, 0)),
                  mod_spec, mod_spec,
                  pl.BlockSpec((1, d), lambda i, j: (0, 0)),
                  pl.BlockSpec((d, COL_BLOCK), lambda i, j: (0, j)),
                  pl.BlockSpec((d, COL_BLOCK), lambda i, j: (0, jnp.minimum(j, 1))),
                  pl.BlockSpec((None, 1, COL_BLOCK), lambda i, j: (jnp.minimum(j, 1), 0, 0)),
                  pl.BlockSpec((COL_BLOCK, COL_BLOCK), lambda i, j: (0, 0))],
        out_specs=pl.BlockSpec((tm, COL_BLOCK), lambda i, j: (i, j)),
        out_shape=jax.ShapeDtypeStruct((n, IN_WIDTH), F32),
        scratch_shapes=[pltpu.VMEM((tm, d), BF16), pltpu.VMEM((tm, d), BF16)],
        compiler_params=_params(("arbitrary", "arbitrary")),
        name="in_proj",
    )(x2d, sc, sh, g, whi, wlo, gqk, bd)


def _third_largest(s):
    m = jnp.max(s, axis=0, keepdims=True)
    for _ in range(MOBA_TOPK - 1):
        s = jnp.where(s >= m, NEG, s)
        m = jnp.max(s, axis=0, keepdims=True)
    return m


def _moba_prompt_kernel(q_ref, k_ref, vt_ref, bias_ref, o_ref, km_ref, selb_ref):
    qi = pl.program_id(2)
    nb, tk, _ = k_ref.shape
    tq = q_ref.shape[0]

    @pl.when(qi == 0)
    def _():
        km_ref[...] = jnp.mean(k_ref[...], axis=1)

    q = q_ref[...]
    lane = lax.broadcasted_iota(jnp.int32, (1, 2 * A_HEAD_DIM), 1)
    head_lanes = [(lane // A_HEAD_DIM) == h for h in range(2)]
    blk = lax.broadcasted_iota(jnp.int32, (nb, tq), 0)
    for h in range(2):
        s = _dot_nt(jnp.where(head_lanes[h], km_ref[...], 0.0), q, HIGHEST)
        s = jnp.where(blk < qi, s, NEG)
        sel = (blk < qi) & (s >= _third_largest(s))
        selb_ref[h] = jnp.where(sel | (blk == qi), 0.0, NEG)

    qs = (q * A_SCALE).astype(BF16)

    def attend(j, dslot, carry):
        kb = k_ref[j]
        vtb = vt_ref[j].astype(BF16)
        out = []
        for h in range(2):
            m, l, acc = carry[h]
            kh = jnp.where(head_lanes[h], kb, 0.0).astype(BF16)
            logits = _dot_nt(kh, qs) + bias_ref[h, dslot] + selb_ref[h, pl.ds(j, 1), :]
            m_new = jnp.maximum(m, jnp.max(logits, axis=0, keepdims=True))
            alpha = jnp.exp(m - m_new)
            p = jnp.exp(logits - m_new)
            l = alpha * l + jnp.sum(p, axis=0, keepdims=True)
            pv = _dot(vtb[h * A_HEAD_DIM:(h + 1) * A_HEAD_DIM, :], p.astype(BF16))
            out.append((m_new, l, alpha * acc + pv))
        return tuple(out)

    init = tuple((jnp.full((1, tq), NEG, F32), jnp.zeros((1, tq), F32), jnp.zeros((A_HEAD_DIM, tq), F32))
                 for _ in range(2))
    carry = attend(qi, 0, init)
    carry = lax.fori_loop(0, qi, lambda j, c: attend(j, jnp.minimum(qi - j, 2), c), carry)
    ot = jnp.concatenate([carry[h][2] / carry[h][1] for h in range(2)], axis=0)
    o_ref[...] = ot.T


def _moba_prompt(z4, vt, bias_t):
    b, nb, tk, _ = z4.shape
    pairs = A_HEADS // 2
    w = 2 * A_HEAD_DIM
    return pl.pallas_call(
        _moba_prompt_kernel,
        grid=(b, pairs, nb),
        in_specs=[pl.BlockSpec((None, None, tk, w), lambda bi, hp, qi: (bi, qi, 0, hp)),
                  pl.BlockSpec((None, nb, tk, w), lambda bi, hp, qi: (bi, 0, 0, pairs + hp)),
                  pl.BlockSpec((None, None, nb, w, tk), lambda bi, hp, qi: (bi, hp, 0, 0, 0)),
                  pl.BlockSpec((2, 3, tk, tk), lambda bi, hp, qi: (hp, 0, 0, 0))],
        out_specs=pl.BlockSpec((None, None, tk, w), lambda bi, hp, qi: (bi, qi, 0, hp)),
        out_shape=jax.ShapeDtypeStruct((b, nb, tk, A_WIDTH), F32),
        scratch_shapes=[pltpu.VMEM((nb, w), F32), pltpu.VMEM((2, nb, tk), F32)],
        compiler_params=_params(("arbitrary", "arbitrary", "arbitrary")),
        name="moba_prompt",
    )(z4, z4, vt, bias_t)


def _hgrn_gates(qr, fr, lb_param):
    e = jnp.exp(lb_param - jnp.max(lb_param, axis=0, keepdims=True))
    lb = e[0:1, :] / jnp.sum(e, axis=0, keepdims=True)
    f = lb + (1.0 - lb) * jax.nn.sigmoid(fr)
    return _silu(qr), f


def _hgrn_prompt_kernel(qr_ref, fr_ref, ir_ref, lb_ref, o_ref, s_ref, st_ref):
    t = qr_ref.shape[0]
    c = R_CHUNK
    row = lax.broadcasted_iota(jnp.int32, (c, c), 0)
    col = lax.broadcasted_iota(jnp.int32, (c, c), 1)
    tril = row >= col
    tril_f = tril.astype(F32)
    lb = lb_ref[...]
    st_ref[...] = jnp.zeros_like(st_ref)

    def chunk(ci, _):
        rows = pl.ds(pl.multiple_of(ci * c, c), c)
        q, f = _hgrn_gates(qr_ref[rows, :], fr_ref[rows, :], lb)
        k = 1.0 - f
        v = ir_ref[rows, :]
        b = _dot(tril_f, jnp.log(f), HIGHEST)
        bl = b[c - 1:c, :]
        qe = (q * jnp.exp(b)).astype(BF16)
        ke = (k * jnp.exp(-b)).astype(BF16)
        kd = (k * jnp.exp(bl - b)).astype(BF16)
        st = st_ref[...]
        att = jnp.where(tril, _dot_nt(qe, ke), 0.0)
        o_ref[rows, :] = _dot_nt(qe, st.astype(BF16)) + _dot(att.astype(BF16), v.astype(BF16))
        st_ref[...] = st * jnp.exp(bl) + _dot(v.T.astype(BF16), kd)
        return 0

    lax.fori_loop(0, t // c, chunk, 0)
    s_ref[...] = st_ref[...].T


def _hgrn_prompt(z2d, lb_param, batch):
    n = z2d.shape[0]
    n_lb = lb_param.shape[0]
    t = n // batch
    w = R_KEY_DIM
    col0 = 3 * A_WIDTH // w
    per = R_WIDTH // w
    spec = lambda off: pl.BlockSpec((t, w), lambda bi, h: (bi, col0 + off * per + h))
    return pl.pallas_call(
        _hgrn_prompt_kernel,
        grid=(batch, R_HEADS),
        in_specs=[spec(0), spec(1), spec(2), pl.BlockSpec((None, n_lb, w), lambda bi, h: (h, 0, 0))],
        out_specs=[pl.BlockSpec((t, w), lambda bi, h: (bi, h)),
                   pl.BlockSpec((None, None, w, w), lambda bi, h: (bi, h, 0, 0))],
        out_shape=[jax.ShapeDtypeStruct((n, R_WIDTH), F32),
                   jax.ShapeDtypeStruct((batch, R_HEADS, R_KEY_DIM, R_VAL_DIM), F32)],
        scratch_shapes=[pltpu.VMEM((R_VAL_DIM, R_KEY_DIM), F32)],
        compiler_params=_params(("arbitrary", "arbitrary")),
        name="hgrn_prompt",
    )(z2d, z2d, z2d, lb_param.reshape(n_lb, R_HEADS, w).transpose(1, 0, 2))


def _merge_kernel(oa_ref, or_ref, gr_ref, ga0_ref, ga1_ref, gb0_ref, gb1_ref, x_ref, gt_ref, go_ref,
                  wa_ref, wr_ref, wo_ref, o_ref):
    o_r = or_ref[...]
    go = go_ref[...]
    heads = []
    for h in range(R_HEADS):
        v = o_r[:, h * R_VAL_DIM:(h + 1) * R_VAL_DIM]
        heads.append(v * lax.rsqrt(jnp.mean(v * v, axis=-1, keepdims=True) + EPS) * go)
    o_r = jnp.concatenate(heads, axis=-1) * _silu(gr_ref[...])
    ga = jnp.concatenate([ga0_ref[...], ga1_ref[...]], axis=-1)
    gb = jnp.concatenate([gb0_ref[...], gb1_ref[...]], axis=-1)
    merged = (jax.nn.sigmoid(ga) * _dot(oa_ref[...].astype(BF16), wa_ref[...])
              + jax.nn.sigmoid(gb) * _dot(o_r.astype(BF16), wr_ref[...]))
    o_ref[...] = x_ref[...] + gt_ref[...] * _dot(merged.astype(BF16), wo_ref[...])


def _merge(o_a, o_r, z2d, x2d, gt, g_o, wa, wr, wo, tm):
    n, d = x2d.shape
    groups, r, _ = gt.shape
    tiles_per_group = n // tm // groups
    zcol = lambda cb: pl.BlockSpec((tm, COL_BLOCK), lambda i: (i, cb))
    full = lambda a: pl.BlockSpec(a.shape, lambda i: (0,) * a.ndim)
    return pl.pallas_call(
        _merge_kernel,
        grid=(n // tm,),
        in_specs=[pl.BlockSpec((tm, A_WIDTH), lambda i: (i, 0)),
                  pl.BlockSpec((tm, R_WIDTH), lambda i: (i, 0)),
                  zcol(6), zcol(7), zcol(8), zcol(9), zcol(10),
                  pl.BlockSpec((tm, d), lambda i: (i, 0)),
                  pl.BlockSpec((None, r, d), lambda i: (i // tiles_per_group, 0, 0)),
                  full(g_o), full(wa), full(wr), full(wo)],
        out_specs=pl.BlockSpec((tm, d), lambda i: (i, 0)),
        out_shape=jax.ShapeDtypeStruct((n, d), F32),
        compiler_params=_params(("arbitrary",)),
        name="merge_out_proj",
    )(o_a, o_r, z2d, z2d, z2d, z2d, z2d, x2d, gt, g_o, wa, wr, wo)


_STAIR = [P_TOPK // (a + 1) for a in range(P_TOPK)]
LANES = 128
E_BLOCK = 2 * P_NKEYS


def _top_rows(s, n):
    rows = []
    for r in range(n):
        m = jnp.max(s, axis=0, keepdims=True)
        rows.append(m)
        if r + 1 < n:
            s = jnp.where(s >= m, NEG, s)
    return rows


def _stack_rows(rows):
    n = len(rows)
    idx = lax.broadcasted_iota(jnp.int32, (n, LANES), 0)
    out = jnp.zeros((n, LANES), F32)
    for r, row in enumerate(rows):
        out = jnp.where(idx == r, row, out)
    return out


def _gelu(x):
    return 0.5 * x * (1.0 + lax.erf(x * (1.0 / math.sqrt(2.0))))


def _peer_kernel(x_ref, sc_ref, sh_ref, gt_ref, g_ref, wpq_ref, sk_ref, u_ref, vt_ref, o_ref,
                 h2_ref, s0_ref, s1_ref, e0_ref, e1_ref, tau_ref, g_buf, acc_ref):
    e = pl.program_id(1)
    tm = x_ref.shape[0]
    nc = tm // LANES

    @pl.when(e == 0)
    def _():
        h2 = _rms_mod(x_ref[...], g_ref[...], sc_ref[...], sh_ref[...]).astype(BF16)
        h2_ref[...] = h2
        q = _dot(h2, wpq_ref[...])
        for h in range(P_HEADS):
            st = _dot_nt(sk_ref[h], q[:, h * P_KEY_DIM:(h + 1) * P_KEY_DIM], HIGHEST)
            for c in range(nc):
                s0_ref[h * nc + c] = st[:P_NKEYS, c * LANES:(c + 1) * LANES]
                s1_ref[h * nc + c] = st[P_NKEYS:, c * LANES:(c + 1) * LANES]

        row8 = lax.broadcasted_iota(jnp.int32, (8, LANES), 0)

        def route(idx, _):
            s0 = s0_ref[idx]
            s1 = s1_ref[idx]
            v0 = _top_rows(s0, P_TOPK)
            v1 = _top_rows(s1, P_TOPK)
            v1_all = _stack_rows(v1)
            cands = [v0[0] + v1_all, v0[1] + v1_all[:8]]
            cands += [jnp.where(row8 < _STAIR[a], v0[a] + v1_all[:8], NEG) for a in range(2, P_TOPK)]
            cand = jnp.concatenate(cands, axis=0)
            tau = _top_rows(cand, P_TOPK)[-1]
            top = v0[0] + v1[0]
            z = jnp.sum(jnp.where(cand >= tau, jnp.exp(cand - top), 0.0), axis=0, keepdims=True)
            e0_ref[idx] = jnp.exp(s0 - v0[0]) / z
            e1_ref[idx] = jnp.exp(s1 - v1[0])
            tau_ref[idx] = jnp.broadcast_to(tau, (8, LANES))
            return 0

        lax.fori_loop(0, P_HEADS * nc, route, 0)
        acc_ref[...] = jnp.zeros_like(acc_ref)

    a_t = _dot_nt(u_ref[...], h2_ref[...])
    for c in range(nc):
        for ii in range(2):
            i = 2 * e + ii
            w = jnp.zeros((P_NKEYS, LANES), F32)
            for h in range(P_HEADS):
                idx = h * nc + c
                t = s0_ref[idx, pl.ds(i, 1), :] + s1_ref[idx]
                w = w + jnp.where(t >= tau_ref[idx, 0:1, :], e1_ref[idx], 0.0) * e0_ref[idx, pl.ds(i, 1), :]
            a = a_t[ii * P_NKEYS:(ii + 1) * P_NKEYS, c * LANES:(c + 1) * LANES]
            g_buf[ii * P_NKEYS:(ii + 1) * P_NKEYS, c * LANES:(c + 1) * LANES] = (_gelu(a) * w).astype(BF16)
    acc_ref[...] += _dot(vt_ref[...], g_buf[...])

    @pl.when(e == pl.num_programs(1) - 1)
    def _():
        o_ref[...] = x_ref[...] + gt_ref[...] * acc_ref[...].T


def _peer(x2d, sc, sh, gt, g, wpq, sk2, u, vt, tm):
    n, d = x2d.shape
    groups, r, _ = sc.shape
    tiles_per_group = n // tm // groups
    nc = tm // LANES
    n_exp = u.shape[0]
    mod_spec = pl.BlockSpec((None, r, d), lambda i, e: (i // tiles_per_group, 0, 0))
    route = pltpu.VMEM((P_HEADS * nc, P_NKEYS, LANES), F32)
    return pl.pallas_call(
        _peer_kernel,
        grid=(n // tm, n_exp // E_BLOCK),
        in_specs=[pl.BlockSpec((tm, d), lambda i, e: (i, 0)),
                  mod_spec, mod_spec, mod_spec,
                  pl.BlockSpec((1, d), lambda i, e: (0, 0)),
                  pl.BlockSpec(wpq.shape, lambda i, e: (0, 0)),
                  pl.BlockSpec(sk2.shape, lambda i, e: (0, 0, 0)),
                  pl.BlockSpec((E_BLOCK, d), lambda i, e: (e, 0)),
                  pl.BlockSpec((d, E_BLOCK), lambda i, e: (0, e))],
        out_specs=pl.BlockSpec((tm, d), lambda i, e: (i, 0)),
        out_shape=jax.ShapeDtypeStruct((n, d), F32),
        scratch_shapes=[pltpu.VMEM((tm, d), BF16), route, route, route, route,
                        pltpu.VMEM((P_HEADS * nc, 8, LANES), F32),
                        pltpu.VMEM((E_BLOCK, tm), BF16),
                        pltpu.VMEM((d, tm), F32)],
        compiler_params=_params(("arbitrary", "arbitrary")),
        name="peer",
    )(x2d, sc, sh, gt, g, wpq, sk2, u, vt)


PAGES_PER_STEP = 16
PAGES_PER_BLOCK = MOBA_BLOCK // PAGE_SIZE


def _cache_means_kernel(pt_ref, *refs):
    o_ref = refs[-1]
    for blk in range(PAGES_PER_STEP // PAGES_PER_BLOCK):
        tot = jnp.zeros((1, o_ref.shape[-1]), F32)
        for p in range(PAGES_PER_BLOCK):
            tot = tot + jnp.sum(refs[blk * PAGES_PER_BLOCK + p][...], axis=0, keepdims=True)
        o_ref[blk:blk + 1, :] = tot * (1.0 / MOBA_BLOCK)


def _cache_means(cache2, pt_flat, batch, n_pages):
    width = cache2.shape[-1]
    steps = n_pages // PAGES_PER_STEP
    page = lambda p: pl.BlockSpec((None, PAGE_SIZE, width),
                                  lambda b, g, pt: (pt[b * n_pages + g * PAGES_PER_STEP + p], 0, 0))
    blocks_per_step = PAGES_PER_STEP // PAGES_PER_BLOCK
    return pl.pallas_call(
        _cache_means_kernel,
        grid_spec=pltpu.PrefetchScalarGridSpec(
            num_scalar_prefetch=1,
            grid=(batch, steps),
            in_specs=[page(p) for p in range(PAGES_PER_STEP)],
            out_specs=pl.BlockSpec((None, blocks_per_step, width), lambda b, g, pt: (b, g, 0))),
        out_shape=jax.ShapeDtypeStruct((batch, n_pages // PAGES_PER_BLOCK, width), F32),
        compiler_params=_params(("arbitrary", "arbitrary")),
        name="cache_block_means",
    )(pt_flat, *([cache2] * PAGES_PER_STEP))


def _select_kernel(q_ref, km_ref, ind_ref, o_ref):
    nb = km_ref.shape[0]
    s = _dot(km_ref[...] * q_ref[...], ind_ref[...], HIGHEST)
    row = lax.broadcasted_iota(jnp.int32, s.shape, 0)
    out_row = lax.broadcasted_iota(jnp.int32, o_ref.shape, 0)
    out = jnp.zeros(o_ref.shape, jnp.int32)
    for r in range(MOBA_TOPK):
        m = jnp.max(s, axis=0, keepdims=True)
        idx = jnp.min(jnp.where(s >= m, row, nb), axis=0, keepdims=True)
        out = jnp.where(out_row == r, idx, out)
        s = jnp.where(row == idx, NEG, s)
    o_ref[...] = out


def _select_blocks(z3, km, ind):
    batch, nb, width = km.shape
    return pl.pallas_call(
        _select_kernel,
        grid=(batch,),
        in_specs=[pl.BlockSpec((None, 1, width), lambda b: (b, 0, 0)),
                  pl.BlockSpec((None, nb, width), lambda b: (b, 0, 0)),
                  pl.BlockSpec(ind.shape, lambda b: (0, 0))],
        out_specs=pl.BlockSpec((None, 8, LANES), lambda b: (b, 0, 0)),
        out_shape=jax.ShapeDtypeStruct((batch, 8, LANES), jnp.int32),
        compiler_params=_params(("arbitrary",)),
        name="moba_select_blocks",
    )(z3, km, ind)


def _moba_sample_kernel(past_len, sel_ref, pt_ref, rb_ref, q_ref, kn_ref, vn_ref, *refs):
    o_ref = refs[-1]
    n_tiles = 2 * MOBA_TOPK * PAGES_PER_BLOCK
    k_refs, v_refs = refs[:n_tiles], refs[n_tiles:2 * n_tiles]
    b = pl.program_id(0)
    hp = pl.program_id(1)
    lane = lax.broadcasted_iota(jnp.int32, (1, LANES), 1)
    row8 = lax.broadcasted_iota(jnp.int32, (8, LANES), 0)
    q = q_ref[...]
    kn = kn_ref[...]
    vn = vn_ref[...]
    res = []
    for h in range(2):
        head = 2 * hp + h
        mine = (lane // A_HEAD_DIM) == h
        qh = jnp.where(mine, q, 0.0) * A_SCALE
        q8 = jnp.where(row8 == 0, qh, 0.0).astype(BF16)
        own = jnp.sum(qh * kn, axis=-1, keepdims=True) + rb_ref[0, head]
        logits = []
        for r in range(MOBA_TOPK):
            blk = sel_ref[(b * A_HEADS + head) * MOBA_TOPK + r]
            for p in range(PAGES_PER_BLOCK):
                t = (h * MOBA_TOPK + r) * PAGES_PER_BLOCK + p
                lg = _dot_nt(q8, k_refs[t][...].astype(BF16))[0:1, :]
                dist = past_len - (blk * MOBA_BLOCK + p * PAGE_SIZE) - lane
                logits.append(lg + _bias_of_distance(dist, rb_ref, head))
        m = own
        for lg in logits:
            m = jnp.maximum(m, jnp.max(lg, axis=-1, keepdims=True))
        p_own = jnp.exp(own - m)
        l = p_own
        acc = p_own * vn
        for t, lg in enumerate(logits):
            p = jnp.exp(lg - m)
            l = l + jnp.sum(p, axis=-1, keepdims=True)
            p8 = jnp.where(row8 == 0, p, 0.0).astype(BF16)
            acc = acc + _dot(p8, v_refs[h * MOBA_TOPK * PAGES_PER_BLOCK + t][...].astype(BF16))[0:1, :]
        res.append((mine, acc / l))
    o_ref[...] = jnp.where(res[0][0], res[0][1], res[1][1])


def _moba_sample(z3, cache_k2, cache_v2, sel_flat, pt_flat, rel_bias, n_pages):
    batch = z3.shape[0]
    pairs = A_HEADS // 2
    w = 2 * A_HEAD_DIM

    def tile(h, r, p):
        def index(b, hp, sel, pt):
            blk = sel[(b * A_HEADS + 2 * hp + h) * MOBA_TOPK + r]
            return (pt[b * n_pages + blk * PAGES_PER_BLOCK + p], 0, hp)
        return pl.BlockSpec((None, PAGE_SIZE, w), index)

    tiles = [tile(h, r, p) for h in range(2) for r in range(MOBA_TOPK) for p in range(PAGES_PER_BLOCK)]
    row = lambda off: pl.BlockSpec((None, 1, w), lambda b, hp, sel, pt: (b, 0, off + hp))
    return pl.pallas_call(
        functools.partial(_moba_sample_kernel, n_pages * PAGE_SIZE),
        grid_spec=pltpu.PrefetchScalarGridSpec(
            num_scalar_prefetch=2,
            grid=(batch, pairs),
            in_specs=[pl.BlockSpec(memory_space=pltpu.SMEM), row(0), row(pairs), row(2 * pairs)] + tiles + tiles,
            out_specs=pl.BlockSpec((None, 1, w), lambda b, hp, sel, pt: (b, 0, hp))),
        out_shape=jax.ShapeDtypeStruct((batch, 1, A_WIDTH), F32),
        compiler_params=_params(("arbitrary", "arbitrary")),
        name="moba_sample",
    )(sel_flat, pt_flat, rel_bias, z3, z3, z3, *([cache_k2] * len(tiles)), *([cache_v2] * len(tiles)))


def _hgrn_sample_kernel(qr_ref, fr_ref, ir_ref, lb_ref, s_ref, o_ref, sn_ref):
    row8 = lax.broadcasted_iota(jnp.int32, (8, LANES), 0)
    eye = (lax.broadcasted_iota(jnp.int32, (LANES, LANES), 0)
           == lax.broadcasted_iota(jnp.int32, (LANES, LANES), 1)).astype(F32)
    outs = []
    for h in range(R_HEADS):
        sl = slice(h * R_KEY_DIM, (h + 1) * R_KEY_DIM)
        q, f = _hgrn_gates(qr_ref[:, sl], fr_ref[:, sl], lb_ref[:, sl])
        k = 1.0 - f
        v = ir_ref[:, sl]
        rows = jnp.where(row8 == 0, q * f, jnp.where(row8 == 1, f, jnp.where(row8 == 2, k, 0.0)))
        cols = _dot_nt(eye, rows, HIGHEST)
        s = s_ref[h]
        outs.append(jnp.sum(s * cols[:, 0:1], axis=0, keepdims=True) + jnp.sum(q * k, axis=-1, keepdims=True) * v)
        sn_ref[h] = s * cols[:, 1:2] + cols[:, 2:3] * v
    o_ref[...] = jnp.concatenate(outs, axis=-1)


def _hgrn_sample(z3, lb_param, state):
    batch = z3.shape[0]
    col = lambda cb: pl.BlockSpec((None, 1, R_WIDTH), lambda b: (b, 0, cb))
    return pl.pallas_call(
        _hgrn_sample_kernel,
        grid=(batch,),
        in_specs=[col(3), col(4), col(5), pl.BlockSpec(lb_param.shape, lambda b: (0, 0)),
                  pl.BlockSpec((None, R_HEADS, R_KEY_DIM, R_VAL_DIM), lambda b: (b, 0, 0, 0))],
        out_specs=[pl.BlockSpec((None, 1, R_WIDTH), lambda b: (b, 0, 0)),
                   pl.BlockSpec((None, R_HEADS, R_KEY_DIM, R_VAL_DIM), lambda b: (b, 0, 0, 0))],
        out_shape=[jax.ShapeDtypeStruct((batch, 1, R_WIDTH), F32),
                   jax.ShapeDtypeStruct(state.shape, F32)],
        compiler_params=_params(("arbitrary",)),
        name="hgrn_sample",
    )(z3, z3, z3, lb_param, state)


def _row_tile(n, preferred):
    t = min(n, preferred)
    while n % t:
        t //= 2
    return t


def kernel(x_prompt, x_sample, c_prompt, c_sample, cache_k, cache_v, state_hgrn, page_table, rel_bias, lb_param, w_ada, b_ada, g_norm1, w_in, g_qnorm, g_knorm, w_a_up, w_r_up, g_onorm, w_out, g_norm2, w_pq, sub_keys, expert_u, expert_v):
    assert w_ada.shape[0] == 1, "single-layer trunk"
    bp, t, d = x_prompt.shape
    bs, ts, _ = x_sample.shape
    n_pages = page_table.shape[1]
    assert ts == 1 and t % MOBA_BLOCK == 0 and n_pages % PAGES_PER_STEP == 0
    assert n_pages // PAGES_PER_BLOCK >= MOBA_TOPK

    w_hi, w_lo = _split_bf16(w_in[0])
    w_lo = w_lo[:, :2 * COL_BLOCK]
    gqk = jnp.stack([jnp.tile(g_qnorm[0], A_HEADS), jnp.tile(g_knorm[0], A_HEADS)]).reshape(2, 1, A_WIDTH)
    head_of = np.arange(A_WIDTH) // A_HEAD_DIM
    bd = jnp.asarray(head_of[:, None] == head_of[None, :], BF16)
    head_ind = jnp.asarray(head_of[:, None] == np.arange(LANES)[None, :], F32)
    wa, wr, wo, wpq = (w[0].astype(BF16) for w in (w_a_up, w_r_up, w_out, w_pq))
    sk2 = jnp.zeros((P_HEADS, 2 * P_NKEYS, P_KEY_DIM), F32)
    sk2 = sk2.at[:, :P_NKEYS, :P_HALF].set(sub_keys[0, :, 0]).at[:, P_NKEYS:, P_HALF:].set(sub_keys[0, :, 1])
    u = expert_u[0].astype(BF16)
    vt = expert_v[0].T.astype(BF16)
    g1, g2, go = g_norm1[0].reshape(1, d), g_norm2[0].reshape(1, d), g_onorm[0].reshape(1, R_VAL_DIM)

    n_c = bp + bs
    c_all = jnp.concatenate([c_prompt, c_sample, jnp.zeros((-n_c % 8, d), F32)], axis=0)
    mod = _ada(c_all, w_ada[0], b_ada[0])
    mod_p = mod[:bp].reshape(bp, 6, 1, d)
    mod_s = mod[bp:n_c].reshape(1, bs, 6, d)
    sh1p, sc1p, gt1p, sh2p, sc2p, gt2p = (mod_p[:, i] for i in range(6))
    sh1s, sc1s, gt1s, sh2s, sc2s, gt2s = (mod_s[:, :, i] for i in range(6))
    bias_t = _bias_tiles(rel_bias)

    xp = x_prompt.reshape(bp * t, d)
    zp = _inproj(xp, sc1p, sh1p, g1, w_hi, w_lo, gqk, bd, _row_tile(t, 1024))
    nb = t // MOBA_BLOCK
    vt_p = zp[:, 2 * A_WIDTH:3 * A_WIDTH].reshape(bp, nb, MOBA_BLOCK, A_HEADS // 2, 2 * A_HEAD_DIM)
    vt_p = vt_p.transpose(0, 3, 1, 4, 2)
    oa_p = _moba_prompt(zp.reshape(bp, nb, MOBA_BLOCK, IN_WIDTH), vt_p, bias_t).reshape(bp * t, A_WIDTH)
    or_p, s_prompt = _hgrn_prompt(zp, lb_param, bp)
    x1p = _merge(oa_p, or_p, zp, xp, gt1p, go, wa, wr, wo, _row_tile(t, 512))
    y_prompt = _peer(x1p, sc2p, sh2p, gt2p, g2, wpq, sk2, u, vt, _row_tile(t, 1024))

    xs = x_sample.reshape(bs, d)
    zs = _inproj(xs, sc1s, sh1s, g1, w_hi, w_lo, gqk, bd, bs)
    z3 = zs.reshape(bs, 1, IN_WIDTH)
    ck2 = cache_k[0].reshape(-1, PAGE_SIZE, A_WIDTH)
    cv2 = cache_v[0].reshape(-1, PAGE_SIZE, A_WIDTH)
    pt_flat = page_table.reshape(-1)
    km = _cache_means(ck2, pt_flat, bs, n_pages)
    sel = _select_blocks(z3, km, head_ind)[:, :MOBA_TOPK, :A_HEADS].transpose(0, 2, 1).reshape(-1)
    oa_s = _moba_sample(z3, ck2, cv2, sel, pt_flat, rel_bias, n_pages).reshape(bs, A_WIDTH)
    or_s, s_sample = _hgrn_sample(z3, lb_param, state_hgrn[0])
    x1s = _merge(oa_s, or_s.reshape(bs, R_WIDTH), zs, xs, gt1s, go, wa, wr, wo, bs)
    y_sample = _peer(x1s, sc2s, sh2s, gt2s, g2, wpq, sk2, u, vt, bs)

    heads = lambda z2, lo, lead: z2[:, lo:lo + A_WIDTH].reshape(1, *lead, A_HEADS, A_HEAD_DIM)
    return (y_prompt.reshape(bp, t, d), y_sample.reshape(bs, 1, d),
            heads(zp, A_WIDTH, (bp, t)), heads(zp, 2 * A_WIDTH, (bp, t)), s_prompt[None],
            heads(zs, A_WIDTH, (bs, 1)), heads(zs, 2 * A_WIDTH, (bs, 1)), s_sample[None])
```

```python
import functools
import math

import numpy as np
import jax
import jax.numpy as jnp
from jax import lax
from jax.experimental import pallas as pl
from jax.experimental.pallas import tpu as pltpu

F32 = jnp.float32
BF16 = jnp.bfloat16
HIGHEST = lax.Precision.HIGHEST

D_MODEL = 1024
EPS = 1e-6
PAGE_SIZE = 128
A_HEADS = 8
A_HEAD_DIM = 64
A_WIDTH = A_HEADS * A_HEAD_DIM
A_SCALE = A_HEAD_DIM ** -0.5
MOBA_BLOCK = 256
MOBA_TOPK = 3
NUM_BUCKETS = 32
MAX_DISTANCE = 128
R_HEADS = 4
R_KEY_DIM = 128
R_VAL_DIM = 128
R_WIDTH = R_HEADS * R_KEY_DIM
R_CHUNK = 32
P_HEADS = 8
P_NKEYS = 128
P_KEY_DIM = 128
P_HALF = P_KEY_DIM // 2
P_TOPK = 16
IN_WIDTH = 3 * A_WIDTH + 4 * R_WIDTH + 2 * D_MODEL
COL_BLOCK = 512
N_COL_BLOCKS = IN_WIDTH // COL_BLOCK
NEG = -1e30
VMEM_LIMIT = 56 * 1024 * 1024


def _dot(a, b, precision=None):
    return jnp.dot(a, b, preferred_element_type=F32, precision=precision)


def _dot_nt(a, b, precision=None):
    return lax.dot_general(a, b, (((1,), (1,)), ((), ())), preferred_element_type=F32, precision=precision)


def _silu(x):
    return x * jax.nn.sigmoid(x)


def _params(sem, vmem=VMEM_LIMIT):
    return pltpu.CompilerParams(dimension_semantics=sem, vmem_limit_bytes=vmem)


def _ada_kernel(c_ref, w_ref, b_ref, o_ref):
    o_ref[...] = _dot(_silu(c_ref[...]), w_ref[...], HIGHEST) + b_ref[...]


def _ada(c, w_ada, b_ada):
    n, d = c.shape
    width = w_ada.shape[1]
    tn = 768
    return pl.pallas_call(
        _ada_kernel,
        grid=(width // tn,),
        in_specs=[pl.BlockSpec((n, d), lambda j: (0, 0)),
                  pl.BlockSpec((d, tn), lambda j: (0, j)),
                  pl.BlockSpec((1, tn), lambda j: (0, j))],
        out_specs=pl.BlockSpec((n, tn), lambda j: (0, j)),
        out_shape=jax.ShapeDtypeStruct((n, width), F32),
        compiler_params=_params(("arbitrary",)),
        name="ada_mod",
    )(c, w_ada, b_ada.reshape(1, width))


def _bucket_starts():
    max_exact = NUM_BUCKETS // 2
    n = np.arange(0, 4 * MAX_DISTANCE)
    nf = np.maximum(n, 1).astype(np.float32)
    large = max_exact + (np.log(nf / np.float32(max_exact)) / np.float32(math.log(MAX_DISTANCE / max_exact))
                         * np.float32(NUM_BUCKETS - max_exact)).astype(np.int32)
    bucket = np.where(n < max_exact, n, np.minimum(large, NUM_BUCKETS - 1))
    assert (np.diff(bucket) >= 0).all() and bucket[-1] == NUM_BUCKETS - 1
    return [int(np.argmax(bucket >= b)) for b in range(NUM_BUCKETS)]


_BUCKET_START = _bucket_starts()


def _bias_of_distance(dist, rb_ref, h):
    val = jnp.full(dist.shape, rb_ref[NUM_BUCKETS - 1, h], F32)
    for b in range(NUM_BUCKETS - 2, -1, -1):
        val = jnp.where(dist < _BUCKET_START[b + 1], rb_ref[b, h], val)
    return val


def _bias_kernel(rb_ref, o_ref):
    h = pl.program_id(0)
    d = pl.program_id(1)
    k = lax.broadcasted_iota(jnp.int32, (MOBA_BLOCK, MOBA_BLOCK), 0)
    q = lax.broadcasted_iota(jnp.int32, (MOBA_BLOCK, MOBA_BLOCK), 1)
    dist = d * MOBA_BLOCK + q - k
    val = _bias_of_distance(dist, rb_ref, h)
    o_ref[...] = jnp.where(dist >= 0, val, NEG)


def _bias_tiles(rel_bias):
    return pl.pallas_call(
        _bias_kernel,
        grid=(A_HEADS, 3),
        in_specs=[pl.BlockSpec(memory_space=pltpu.SMEM)],
        out_specs=pl.BlockSpec((None, None, MOBA_BLOCK, MOBA_BLOCK), lambda h, d: (h, d, 0, 0)),
        out_shape=jax.ShapeDtypeStruct((A_HEADS, 3, MOBA_BLOCK, MOBA_BLOCK), F32),
        compiler_params=_params(("arbitrary", "arbitrary")),
        name="moba_bias_tiles",
    )(rel_bias)


def _rms_mod(x, g, sc, sh):
    ms = jnp.mean(x * x, axis=-1, keepdims=True)
    return (x * lax.rsqrt(ms + EPS) * g) * (1.0 + sc) + sh


def _split_bf16(x):
    hi = x.astype(BF16)
    return hi, (x - hi.astype(F32)).astype(BF16)


def _inproj_kernel(x_ref, sc_ref, sh_ref, g_ref, whi_ref, wlo_ref, gqk_ref, bd_ref, o_ref, hhi_ref, hlo_ref):
    j = pl.program_id(1)

    @pl.when(j == 0)
    def _():
        hhi, hlo = _split_bf16(_rms_mod(x_ref[...], g_ref[...], sc_ref[...], sh_ref[...]))
        hhi_ref[...] = hhi
        hlo_ref[...] = hlo

    @pl.when(j < 2)
    def _():
        hhi = hhi_ref[...]
        whi = whi_ref[...]
        z = _dot(hhi, whi) + (_dot(hhi, wlo_ref[...]) + _dot(hlo_ref[...], whi))
        zhi, zlo = _split_bf16(z * z)
        ssq = _dot(zhi, bd_ref[...]) + _dot(zlo, bd_ref[...])
        o_ref[...] = z * lax.rsqrt(ssq * (1.0 / A_HEAD_DIM) + EPS) * gqk_ref[...]

    @pl.when(j >= 2)
    def _():
        o_ref[...] = _dot(hhi_ref[...], whi_ref[...])


def _inproj(x2d, sc, sh, g, whi, wlo, gqk, bd, tm):
    n, d = x2d.shape
    groups, r, _ = sc.shape
    tiles_per_group = n // tm // groups
    mod_spec = pl.BlockSpec((None, r, d), lambda i, j: (i // tiles_per_group, 0, 0))
    return pl.pallas_call(
        _inproj_kernel,
        grid=(n // tm, N_COL_BLOCKS),
        in_specs=[pl.BlockSpec((tm, d), lambda i, j: (i, 0)),
                  mod_spec, mod_spec,
                  pl.BlockSpec((1, d), lambda i, j: (0, 0)),
                  pl.BlockSpec((d, COL_BLOCK), lambda i, j: (0, j)),
                  pl.BlockSpec((d, COL_BLOCK), lambda i, j: (0, jnp.minimum(j, 1))),
                  pl.BlockSpec((None, 1, COL_BLOCK), lambda i, j: (jnp.minimum(j, 1), 0, 0)),
                  pl.BlockSpec((COL_BLOCK, COL_BLOCK), lambda i, j: (0, 0))],
        out_specs=pl.BlockSpec((tm, COL_BLOCK), lambda i, j: (i, j)),
        out_shape=jax.ShapeDtypeStruct((n, IN_WIDTH), F32),
        scratch_shapes=[pltpu.VMEM((tm, d), BF16), pltpu.VMEM((tm, d), BF16)],
        compiler_params=_params(("arbitrary", "arbitrary")),
        name="in_proj",
    )(x2d, sc, sh, g, whi, wlo, gqk, bd)


def _third_largest(s):
    m = jnp.max(s, axis=0, keepdims=True)
    for _ in range(MOBA_TOPK - 1):
        s = jnp.where(s >= m, NEG, s)
        m = jnp.max(s, axis=0, keepdims=True)
    return m


MOBA_HEADS_PER_STEP = 4
MOBA_GROUP_WIDTH = MOBA_HEADS_PER_STEP * A_HEAD_DIM


def _moba_prompt_kernel(q_ref, k_ref, vt_ref, bias_ref, o_ref, km_ref, selb_ref):
    qi = pl.program_id(2)
    nb, tk, _ = k_ref.shape
    tq = q_ref.shape[0]
    nh = MOBA_HEADS_PER_STEP

    @pl.when(qi == 0)
    def _():
        km_ref[...] = jnp.mean(k_ref[...], axis=1)

    q = q_ref[...]
    lane = lax.broadcasted_iota(jnp.int32, (1, MOBA_GROUP_WIDTH), 1)
    head_lanes = [(lane // A_HEAD_DIM) == h for h in range(nh)]
    blk = lax.broadcasted_iota(jnp.int32, (nb, tq), 0)
    for h in range(nh):
        s = _dot_nt(jnp.where(head_lanes[h], km_ref[...], 0.0), q, HIGHEST)
        s = jnp.where(blk < qi, s, NEG)
        sel = (blk < qi) & (s >= _third_largest(s))
        selb_ref[h] = jnp.where(sel | (blk == qi), 0.0, NEG)

    qs = (q * A_SCALE).astype(BF16)

    def attend(j, dslot, carry):
        kb = k_ref[j]
        vtb = vt_ref[j].astype(BF16)
        out = []
        for h in range(nh):
            m, l, acc = carry[h]
            kh = jnp.where(head_lanes[h], kb, 0.0).astype(BF16)
            logits = _dot_nt(kh, qs) + bias_ref[h, dslot] + selb_ref[h, pl.ds(j, 1), :]
            m_new = jnp.maximum(m, jnp.max(logits, axis=0, keepdims=True))
            alpha = jnp.exp(m - m_new)
            p = jnp.exp(logits - m_new)
            l = alpha * l + jnp.sum(p, axis=0, keepdims=True)
            pv = _dot(vtb[h * A_HEAD_DIM:(h + 1) * A_HEAD_DIM, :], p.astype(BF16))
            out.append((m_new, l, alpha * acc + pv))
        return tuple(out)

    past = lambda j, c: attend(j, jnp.minimum(qi - j, 2), c)
    init = tuple((jnp.full((1, tq), NEG, F32), jnp.zeros((1, tq), F32), jnp.zeros((A_HEAD_DIM, tq), F32))
                 for _ in range(nh))
    carry = attend(qi, 0, init)
    carry = lax.fori_loop(0, qi // 2, lambda i, c: past(2 * i + 1, past(2 * i, c)), carry)
    carry = lax.cond(qi % 2 == 1, lambda c: past(qi - 1, c), lambda c: c, carry)
    ot = jnp.concatenate([carry[h][2] / carry[h][1] for h in range(nh)], axis=0)
    o_ref[...] = ot.T


def _moba_prompt(z4, vt, bias_t):
    b, nb, tk, _ = z4.shape
    groups = A_HEADS // MOBA_HEADS_PER_STEP
    w = MOBA_GROUP_WIDTH
    return pl.pallas_call(
        _moba_prompt_kernel,
        grid=(b, groups, nb),
        in_specs=[pl.BlockSpec((None, None, tk, w), lambda bi, g, qi: (bi, qi, 0, g)),
                  pl.BlockSpec((None, nb, tk, w), lambda bi, g, qi: (bi, 0, 0, groups + g)),
                  pl.BlockSpec((None, None, nb, w, tk), lambda bi, g, qi: (bi, g, 0, 0, 0)),
                  pl.BlockSpec((MOBA_HEADS_PER_STEP, 3, tk, tk), lambda bi, g, qi: (g, 0, 0, 0))],
        out_specs=pl.BlockSpec((None, None, tk, w), lambda bi, g, qi: (bi, qi, 0, g)),
        out_shape=jax.ShapeDtypeStruct((b, nb, tk, A_WIDTH), F32),
        scratch_shapes=[pltpu.VMEM((nb, w), F32), pltpu.VMEM((MOBA_HEADS_PER_STEP, nb, tk), F32)],
        compiler_params=_params(("arbitrary", "arbitrary", "arbitrary")),
        name="moba_prompt",
    )(z4, z4, vt, bias_t)


def _hgrn_gates(qr, fr, lb_param):
    e = jnp.exp(lb_param - jnp.max(lb_param, axis=0, keepdims=True))
    lb = e[0:1, :] / jnp.sum(e, axis=0, keepdims=True)
    f = lb + (1.0 - lb) * jax.nn.sigmoid(fr)
    return _silu(qr), f


def _hgrn_prompt_kernel(qr_ref, fr_ref, ir_ref, lb_ref, o_ref, s_ref, st_ref):
    tb = pl.program_id(1)
    t = qr_ref.shape[0]
    c = R_CHUNK
    row = lax.broadcasted_iota(jnp.int32, (c, c), 0)
    col = lax.broadcasted_iota(jnp.int32, (c, c), 1)
    tril = row >= col
    tril_f = tril.astype(F32)
    lb = lb_ref[...]

    @pl.when(tb == 0)
    def _():
        st_ref[...] = jnp.zeros_like(st_ref)

    def chunk(ci, _):
        rows = pl.ds(pl.multiple_of(ci * c, c), c)
        q, f = _hgrn_gates(qr_ref[rows, :], fr_ref[rows, :], lb)
        k = 1.0 - f
        v = ir_ref[rows, :]
        b = _dot(tril_f, jnp.log(f), HIGHEST)
        bl = b[c - 1:c, :]
        qe = (q * jnp.exp(b)).astype(BF16)
        ke = (k * jnp.exp(-b)).astype(BF16)
        kd = (k * jnp.exp(bl - b)).astype(BF16)
        decay = jnp.exp(bl)
        outs = []
        for h in range(R_HEADS):
            sl = slice(h * R_KEY_DIM, (h + 1) * R_KEY_DIM)
            st = st_ref[h]
            vh = v[:, sl]
            att = jnp.where(tril, _dot_nt(qe[:, sl], ke[:, sl]), 0.0)
            outs.append(_dot_nt(qe[:, sl], st.astype(BF16)) + _dot(att.astype(BF16), vh.astype(BF16)))
            st_ref[h] = st * decay[:, sl] + _dot(vh.T.astype(BF16), kd[:, sl])
        o_ref[rows, :] = jnp.concatenate(outs, axis=-1)
        return 0

    lax.fori_loop(0, t // c, chunk, 0, unroll=2)

    @pl.when(tb == pl.num_programs(1) - 1)
    def _():
        for h in range(R_HEADS):
            s_ref[h] = st_ref[h].T


def _hgrn_prompt(z2d, lb_param, batch):
    n = z2d.shape[0]
    t = n // batch
    tb = _row_tile(t, 1024)
    col0 = 3 * A_WIDTH // R_WIDTH
    spec = lambda off: pl.BlockSpec((tb, R_WIDTH), lambda bi, ti: (bi * (t // tb) + ti, col0 + off))
    state = pl.BlockSpec((None, R_HEADS, R_KEY_DIM, R_VAL_DIM), lambda bi, ti: (bi, 0, 0, 0))
    return pl.pallas_call(
        _hgrn_prompt_kernel,
        grid=(batch, t // tb),
        in_specs=[spec(0), spec(1), spec(2), pl.BlockSpec(lb_param.shape, lambda bi, ti: (0, 0))],
        out_specs=[pl.BlockSpec((tb, R_WIDTH), lambda bi, ti: (bi * (t // tb) + ti, 0)), state],
        out_shape=[jax.ShapeDtypeStruct((n, R_WIDTH), F32),
                   jax.ShapeDtypeStruct((batch, R_HEADS, R_KEY_DIM, R_VAL_DIM), F32)],
        scratch_shapes=[pltpu.VMEM((R_HEADS, R_VAL_DIM, R_KEY_DIM), F32)],
        compiler_params=_params(("arbitrary", "arbitrary")),
        name="hgrn_prompt",
    )(z2d, z2d, z2d, lb_param)


def _merge_kernel(oa_ref, or_ref, gr_ref, ga0_ref, ga1_ref, gb0_ref, gb1_ref, x_ref, gt_ref, go_ref,
                  wa_ref, wr_ref, wo_ref, o_ref):
    o_r = or_ref[...]
    go = go_ref[...]
    heads = []
    for h in range(R_HEADS):
        v = o_r[:, h * R_VAL_DIM:(h + 1) * R_VAL_DIM]
        heads.append(v * lax.rsqrt(jnp.mean(v * v, axis=-1, keepdims=True) + EPS) * go)
    o_r = jnp.concatenate(heads, axis=-1) * _silu(gr_ref[...])
    ga = jnp.concatenate([ga0_ref[...], ga1_ref[...]], axis=-1)
    gb = jnp.concatenate([gb0_ref[...], gb1_ref[...]], axis=-1)
    merged = (jax.nn.sigmoid(ga) * _dot(oa_ref[...].astype(BF16), wa_ref[...])
              + jax.nn.sigmoid(gb) * _dot(o_r.astype(BF16), wr_ref[...]))
    o_ref[...] = x_ref[...] + gt_ref[...] * _dot(merged.astype(BF16), wo_ref[...])


def _merge(o_a, o_r, z2d, x2d, gt, g_o, wa, wr, wo, tm):
    n, d = x2d.shape
    groups, r, _ = gt.shape
    tiles_per_group = n // tm // groups
    zcol = lambda cb: pl.BlockSpec((tm, COL_BLOCK), lambda i: (i, cb))
    full = lambda a: pl.BlockSpec(a.shape, lambda i: (0,) * a.ndim)
    return pl.pallas_call(
        _merge_kernel,
        grid=(n // tm,),
        in_specs=[pl.BlockSpec((tm, A_WIDTH), lambda i: (i, 0)),
                  pl.BlockSpec((tm, R_WIDTH), lambda i: (i, 0)),
                  zcol(6), zcol(7), zcol(8), zcol(9), zcol(10),
                  pl.BlockSpec((tm, d), lambda i: (i, 0)),
                  pl.BlockSpec((None, r, d), lambda i: (i // tiles_per_group, 0, 0)),
                  full(g_o), full(wa), full(wr), full(wo)],
        out_specs=pl.BlockSpec((tm, d), lambda i: (i, 0)),
        out_shape=jax.ShapeDtypeStruct((n, d), F32),
        compiler_params=_params(("arbitrary",)),
        name="merge_out_proj",
    )(o_a, o_r, z2d, z2d, z2d, z2d, z2d, x2d, gt, g_o, wa, wr, wo)


_STAIR = [P_TOPK // (a + 1) for a in range(P_TOPK)]
LANES = 128
ROWS_PER_BLOCK = 4
E_BLOCK = ROWS_PER_BLOCK * P_NKEYS


def _top_rows(s, n):
    rows = []
    for r in range(n):
        m = jnp.max(s, axis=0, keepdims=True)
        rows.append(m)
        if r + 1 < n:
            s = jnp.where(s >= m, NEG, s)
    return rows


def _stack_rows(rows):
    n = len(rows)
    idx = lax.broadcasted_iota(jnp.int32, (n, LANES), 0)
    out = jnp.zeros((n, LANES), F32)
    for r, row in enumerate(rows):
        out = jnp.where(idx == r, row, out)
    return out


def _top_rows_ranked(s, n):
    rows = []
    rank = jnp.full(s.shape, float(n), F32)
    for r in range(n):
        m = jnp.max(s, axis=0, keepdims=True)
        rows.append(m)
        hit = s >= m
        rank = jnp.where(hit, float(r), rank)
        s = jnp.where(hit, NEG, s)
    return rows, rank


def _route(s0, s1):
    row8 = lax.broadcasted_iota(jnp.int32, (8, LANES), 0)
    v0 = _top_rows(s0, P_TOPK)
    v1, rank1 = _top_rows_ranked(s1, P_TOPK)
    v1_all = _stack_rows(v1)
    cands = [v0[0] + v1_all, v0[1] + v1_all[:8]]
    cands += [jnp.where(row8 < _STAIR[a], v0[a] + v1_all[:8], NEG) for a in range(2, P_TOPK)]
    cand = jnp.concatenate(cands, axis=0)
    tau = _top_rows(cand, P_TOPK)[-1]
    top = v0[0] + v1[0]
    z = jnp.sum(jnp.where(cand >= tau, jnp.exp(cand - top), 0.0), axis=0, keepdims=True)
    n = jnp.zeros(s0.shape, F32)
    for a in range(P_TOPK):
        n_a = jnp.sum(jnp.where(cands[a] >= tau, 1.0, 0.0), axis=0, keepdims=True)
        n = jnp.where(s0 == v0[a], n_a, n)
    return n, jnp.exp(s0 - v0[0]) / z, rank1.astype(BF16), jnp.exp(s1 - v1[0]).astype(BF16)


def _peer_kernel(x_ref, sc_ref, sh_ref, gt_ref, g_ref, wpq_ref, skh_ref, skl_ref, u0_ref, ua_ref, ub_ref,
                 vta_ref, vtb_ref, o_ref, h2_ref, n_ref, e0_ref, r1_ref, e1_ref, a0_ref, a1_ref, g0_ref, g1_ref,
                 acc_ref):
    s = pl.program_id(1)
    tm = x_ref.shape[0]
    nc = tm // LANES
    group = min(nc, 2)

    @pl.when(s == 0)
    def _():
        h2 = _rms_mod(x_ref[...], g_ref[...], sc_ref[...], sh_ref[...]).astype(BF16)
        h2_ref[...] = h2
        q = _dot(h2, wpq_ref[...])
        for h in range(P_HEADS):
            qh, ql = _split_bf16(q[:, h * P_KEY_DIM:(h + 1) * P_KEY_DIM])
            skh = skh_ref[h]
            st = _dot_nt(skh, qh) + (_dot_nt(skh, ql) + _dot_nt(skl_ref[h], qh))
            for c in range(nc):
                n_ref[h * nc + c] = st[:P_NKEYS, c * LANES:(c + 1) * LANES]
                e0_ref[h * nc + c] = st[P_NKEYS:, c * LANES:(c + 1) * LANES]

        def route(pair, _):
            for idx in (2 * pair, 2 * pair + 1):
                n, e0, r1, e1 = _route(n_ref[idx], e0_ref[idx])
                n_ref[idx] = n
                e0_ref[idx] = 0.5 * e0
                r1_ref[idx] = r1.reshape(P_NKEYS // 16, 16, LANES)
                e1_ref[idx] = e1.reshape(P_NKEYS // 16, 16, LANES)
            return 0

        lax.fori_loop(0, P_HEADS * nc // 2, route, 0)
        acc_ref[...] = jnp.zeros_like(acc_ref)
        a0_ref[...] = _dot_nt(u0_ref[...], h2)

    def gate(blk, a_ref, g_ref_, c):
        w = [jnp.zeros((P_NKEYS // 16, 16, LANES), BF16) for _ in range(ROWS_PER_BLOCK)]
        for h in range(P_HEADS):
            idx = h * nc + c
            r1 = r1_ref[idx]
            e1 = e1_ref[idx]
            for ii in range(ROWS_PER_BLOCK):
                i = ROWS_PER_BLOCK * blk + ii
                n_row = jnp.broadcast_to(n_ref[idx, pl.ds(i, 1), :], (16, LANES)).astype(BF16)
                e0_row = jnp.broadcast_to(e0_ref[idx, pl.ds(i, 1), :], (16, LANES)).astype(BF16)
                w[ii] = w[ii] + jnp.where(r1 < n_row[None], e1, jnp.zeros((), BF16)) * e0_row[None]
        for ii in range(ROWS_PER_BLOCK):
            a = a_ref[ii * P_NKEYS:(ii + 1) * P_NKEYS, c * LANES:(c + 1) * LANES]
            act = (a * (1.0 + lax.erf(a * (1.0 / math.sqrt(2.0))))).astype(BF16)
            g_ref_[ii * P_NKEYS:(ii + 1) * P_NKEYS, c * LANES:(c + 1) * LANES] = act * w[ii].reshape(P_NKEYS, LANES)

    def half_step(blk, a_src, a_dst, u_next, g_dst, vt_cur):
        for cg in range(nc // group):
            cols = slice(cg * group * LANES, (cg + 1) * group * LANES)
            a_dst[:, cols] = _dot_nt(u_next[...], h2_ref[cols, :])
            for c in range(cg * group, (cg + 1) * group):
                gate(blk, a_src, g_dst, c)
            acc_ref[:, cols] += _dot(vt_cur[...], g_dst[:, cols])

    half_step(2 * s, a0_ref, a1_ref, ua_ref, g0_ref, vta_ref)
    half_step(2 * s + 1, a1_ref, a0_ref, ub_ref, g1_ref, vtb_ref)

    @pl.when(s == pl.num_programs(1) - 1)
    def _():
        o_ref[...] = x_ref[...] + gt_ref[...] * acc_ref[...].T


def _peer(x2d, sc, sh, gt, g, wpq, skh, skl, u, vt, tm):
    n, d = x2d.shape
    groups, r, _ = sc.shape
    tiles_per_group = n // tm // groups
    nc = tm // LANES
    n_blk = u.shape[0] // E_BLOCK
    mod_spec = pl.BlockSpec((None, r, d), lambda i, s: (i // tiles_per_group, 0, 0))
    route32 = pltpu.VMEM((P_HEADS * nc, P_NKEYS, LANES), F32)
    route16 = pltpu.VMEM((P_HEADS * nc, P_NKEYS // 16, 16, LANES), BF16)
    pre = pltpu.VMEM((E_BLOCK, tm), F32)
    gated = pltpu.VMEM((E_BLOCK, tm), BF16)
    single = pl.Buffered(1)
    return pl.pallas_call(
        _peer_kernel,
        grid=(n // tm, n_blk // 2),
        in_specs=[pl.BlockSpec((tm, d), lambda i, s: (i, 0), pipeline_mode=single),
                  mod_spec, mod_spec, mod_spec,
                  pl.BlockSpec((1, d), lambda i, s: (0, 0)),
                  pl.BlockSpec(wpq.shape, lambda i, s: (0, 0), pipeline_mode=single),
                  pl.BlockSpec(skh.shape, lambda i, s: (0, 0, 0), pipeline_mode=single),
                  pl.BlockSpec(skl.shape, lambda i, s: (0, 0, 0), pipeline_mode=single),
                  pl.BlockSpec((E_BLOCK, d), lambda i, s: (0, 0), pipeline_mode=single),
                  pl.BlockSpec((E_BLOCK, d), lambda i, s: (2 * s + 1, 0)),
                  pl.BlockSpec((E_BLOCK, d), lambda i, s: (jnp.minimum(2 * s + 2, n_blk - 1), 0)),
                  pl.BlockSpec((d, E_BLOCK), lambda i, s: (0, 2 * s)),
                  pl.BlockSpec((d, E_BLOCK), lambda i, s: (0, 2 * s + 1))],
        out_specs=pl.BlockSpec((tm, d), lambda i, s: (i, 0), pipeline_mode=single),
        out_shape=jax.ShapeDtypeStruct((n, d), F32),
        scratch_shapes=[pltpu.VMEM((tm, d), BF16), route32, route32, route16, route16, pre, pre, gated, gated,
                        pltpu.VMEM((d, tm), F32)],
        compiler_params=_params(("arbitrary", "arbitrary")),
        name="peer",
    )(x2d, sc, sh, gt, g, wpq, skh, skl, u, u, u, vt, vt)


PAGES_PER_STEP = 16
PAGES_PER_BLOCK = MOBA_BLOCK // PAGE_SIZE


def _cache_means_kernel(pt_ref, *refs):
    o_ref = refs[-1]
    for blk in range(PAGES_PER_STEP // PAGES_PER_BLOCK):
        tot = jnp.zeros(o_ref.shape[1:], F32)
        for p in range(PAGES_PER_BLOCK):
            tot = tot + jnp.sum(refs[blk * PAGES_PER_BLOCK + p][...], axis=0)
        o_ref[blk] = tot * (1.0 / MOBA_BLOCK)


def _cache_means(cache4, pt_flat, batch, n_pages):
    _, _, nh, dh = cache4.shape
    steps = n_pages // PAGES_PER_STEP
    page = lambda p: pl.BlockSpec((None, PAGE_SIZE, nh, dh),
                                  lambda b, g, pt: (pt[b * n_pages + g * PAGES_PER_STEP + p], 0, 0, 0))
    blocks_per_step = PAGES_PER_STEP // PAGES_PER_BLOCK
    return pl.pallas_call(
        _cache_means_kernel,
        grid_spec=pltpu.PrefetchScalarGridSpec(
            num_scalar_prefetch=1,
            grid=(batch, steps),
            in_specs=[page(p) for p in range(PAGES_PER_STEP)],
            out_specs=pl.BlockSpec((None, blocks_per_step, nh, dh), lambda b, g, pt: (b, g, 0, 0))),
        out_shape=jax.ShapeDtypeStruct((batch, n_pages // PAGES_PER_BLOCK, nh, dh), F32),
        compiler_params=_params(("arbitrary", "arbitrary")),
        name="cache_block_means",
    )(pt_flat, *([cache4] * PAGES_PER_STEP))


def _select_kernel(q_ref, km_ref, ind_ref, o_ref):
    nb = km_ref.shape[0]
    s = _dot(km_ref[...] * q_ref[...], ind_ref[...], HIGHEST)
    row = lax.broadcasted_iota(jnp.int32, s.shape, 0)
    out_row = lax.broadcasted_iota(jnp.int32, o_ref.shape, 0)
    out = jnp.zeros(o_ref.shape, jnp.int32)
    for r in range(MOBA_TOPK):
        m = jnp.max(s, axis=0, keepdims=True)
        idx = jnp.min(jnp.where(s >= m, row, nb), axis=0, keepdims=True)
        out = jnp.where(out_row == r, idx, out)
        s = jnp.where(row == idx, NEG, s)
    o_ref[...] = out


def _select_blocks(z3, km, ind):
    batch, nb, width = km.shape
    return pl.pallas_call(
        _select_kernel,
        grid=(batch,),
        in_specs=[pl.BlockSpec((None, 1, width), lambda b: (b, 0, 0)),
                  pl.BlockSpec((None, nb, width), lambda b: (b, 0, 0)),
                  pl.BlockSpec(ind.shape, lambda b: (0, 0))],
        out_specs=pl.BlockSpec((None, 8, LANES), lambda b: (b, 0, 0)),
        out_shape=jax.ShapeDtypeStruct((batch, 8, LANES), jnp.int32),
        compiler_params=_params(("arbitrary",)),
        name="moba_select_blocks",
    )(z3, km, ind)


TILES = MOBA_TOPK * PAGES_PER_BLOCK


def _moba_sample_kernel(past_len, n_pages, sel_ref, pt_ref, rb_ref, q_ref, kn_ref, vn_ref, ck_ref, cv_ref, o_ref,
                        kbuf, vbuf, sem):
    b = pl.program_id(0)
    nb = pl.num_programs(0)

    def copies(bb, slot):
        out = []
        for h in range(A_HEADS):
            for r in range(MOBA_TOPK):
                blk = sel_ref[(bb * A_HEADS + h) * MOBA_TOPK + r]
                for p in range(PAGES_PER_BLOCK):
                    page = pt_ref[bb * n_pages + blk * PAGES_PER_BLOCK + p]
                    t = r * PAGES_PER_BLOCK + p
                    out.append(pltpu.make_async_copy(ck_ref.at[page, :, h, :], kbuf.at[slot, h, t], sem.at[slot]))
                    out.append(pltpu.make_async_copy(cv_ref.at[page, :, h, :], vbuf.at[slot, h, t], sem.at[slot]))
        return out

    @pl.when(b == 0)
    def _():
        for c in copies(0, 0):
            c.start()

    @pl.when(b + 1 < nb)
    def _():
        for c in copies(b + 1, (b + 1) % 2):
            c.start()

    slot = b % 2
    for c in copies(b, slot):
        c.wait()

    row8 = lax.broadcasted_iota(jnp.int32, (8, A_HEAD_DIM), 0)
    row8k = lax.broadcasted_iota(jnp.int32, (8, TILES * PAGE_SIZE), 0)
    lane = lax.broadcasted_iota(jnp.int32, (1, TILES * PAGE_SIZE), 1)
    for h in range(A_HEADS):
        qh = q_ref[h:h + 1, :] * A_SCALE
        kn = kn_ref[h:h + 1, :]
        vn = vn_ref[h:h + 1, :]
        q8 = jnp.where(row8 == 0, qh, 0.0).astype(BF16)
        keys = kbuf[slot, h].reshape(TILES * PAGE_SIZE, A_HEAD_DIM).astype(BF16)
        vals = vbuf[slot, h].reshape(TILES * PAGE_SIZE, A_HEAD_DIM).astype(BF16)
        start = [past_len - sel_ref[(b * A_HEADS + h) * MOBA_TOPK + r] * MOBA_BLOCK for r in range(MOBA_TOPK)]
        which = lane // MOBA_BLOCK
        dist = jnp.where(which == 0, start[0], jnp.where(which == 1, start[1], start[2])) - lane % MOBA_BLOCK
        logits = _dot_nt(q8, keys)[0:1, :] + _bias_of_distance(dist, rb_ref, h)
        own = jnp.sum(qh * kn, axis=-1, keepdims=True) + rb_ref[0, h]
        m = jnp.maximum(own, jnp.max(logits, axis=-1, keepdims=True))
        p_own = jnp.exp(own - m)
        p = jnp.exp(logits - m)
        l = p_own + jnp.sum(p, axis=-1, keepdims=True)
        p8 = jnp.where(row8k == 0, p, 0.0).astype(BF16)
        o_ref[h:h + 1, :] = (_dot(p8, vals)[0:1, :] + p_own * vn) / l


def _moba_sample(q3, kn3, vn3, cache_k4, cache_v4, sel_flat, pt_flat, rel_bias, n_pages):
    batch = q3.shape[0]
    row = pl.BlockSpec((None, A_HEADS, A_HEAD_DIM), lambda b, sel, pt: (b, 0, 0))
    buf = pltpu.VMEM((2, A_HEADS, TILES, PAGE_SIZE, A_HEAD_DIM), F32)
    return pl.pallas_call(
        functools.partial(_moba_sample_kernel, n_pages * PAGE_SIZE, n_pages),
        grid_spec=pltpu.PrefetchScalarGridSpec(
            num_scalar_prefetch=2,
            grid=(batch,),
            in_specs=[pl.BlockSpec(memory_space=pltpu.SMEM), row, row, row,
                      pl.BlockSpec(memory_space=pl.ANY), pl.BlockSpec(memory_space=pl.ANY)],
            out_specs=row,
            scratch_shapes=[buf, buf, pltpu.SemaphoreType.DMA((2,))]),
        out_shape=jax.ShapeDtypeStruct((batch, A_HEADS, A_HEAD_DIM), F32),
        compiler_params=_params(("arbitrary",)),
        name="moba_sample",
    )(sel_flat, pt_flat, rel_bias, q3, kn3, vn3, cache_k4, cache_v4)


def _hgrn_sample_kernel(qr_ref, fr_ref, ir_ref, lb_ref, s_ref, o_ref, sn_ref):
    row8 = lax.broadcasted_iota(jnp.int32, (8, LANES), 0)
    eye = (lax.broadcasted_iota(jnp.int32, (LANES, LANES), 0)
           == lax.broadcasted_iota(jnp.int32, (LANES, LANES), 1)).astype(F32)
    outs = []
    for h in range(R_HEADS):
        sl = slice(h * R_KEY_DIM, (h + 1) * R_KEY_DIM)
        q, f = _hgrn_gates(qr_ref[:, sl], fr_ref[:, sl], lb_ref[:, sl])
        k = 1.0 - f
        v = ir_ref[:, sl]
        rows = jnp.where(row8 == 0, q * f, jnp.where(row8 == 1, f, jnp.where(row8 == 2, k, 0.0)))
        cols = _dot_nt(eye, rows, HIGHEST)
        s = s_ref[h]
        outs.append(jnp.sum(s * cols[:, 0:1], axis=0, keepdims=True) + jnp.sum(q * k, axis=-1, keepdims=True) * v)
        sn_ref[h] = s * cols[:, 1:2] + cols[:, 2:3] * v
    o_ref[...] = jnp.concatenate(outs, axis=-1)


def _hgrn_sample(z3, lb_param, state):
    batch = z3.shape[0]
    col = lambda cb: pl.BlockSpec((None, 1, R_WIDTH), lambda b: (b, 0, cb))
    return pl.pallas_call(
        _hgrn_sample_kernel,
        grid=(batch,),
        in_specs=[col(3), col(4), col(5), pl.BlockSpec(lb_param.shape, lambda b: (0, 0)),
                  pl.BlockSpec((None, R_HEADS, R_KEY_DIM, R_VAL_DIM), lambda b: (b, 0, 0, 0))],
        out_specs=[pl.BlockSpec((None, 1, R_WIDTH), lambda b: (b, 0, 0)),
                   pl.BlockSpec((None, R_HEADS, R_KEY_DIM, R_VAL_DIM), lambda b: (b, 0, 0, 0))],
        out_shape=[jax.ShapeDtypeStruct((batch, 1, R_WIDTH), F32),
                   jax.ShapeDtypeStruct(state.shape, F32)],
        compiler_params=_params(("arbitrary",)),
        name="hgrn_sample",
    )(z3, z3, z3, lb_param, state)


def _row_tile(n, preferred):
    t = min(n, preferred)
    while n % t:
        t //= 2
    return t


def kernel(x_prompt, x_sample, c_prompt, c_sample, cache_k, cache_v, state_hgrn, page_table, rel_bias, lb_param, w_ada, b_ada, g_norm1, w_in, g_qnorm, g_knorm, w_a_up, w_r_up, g_onorm, w_out, g_norm2, w_pq, sub_keys, expert_u, expert_v):
    assert w_ada.shape[0] == 1, "single-layer trunk"
    bp, t, d = x_prompt.shape
    bs, ts, _ = x_sample.shape
    n_pages = page_table.shape[1]
    assert ts == 1 and t % MOBA_BLOCK == 0 and n_pages % PAGES_PER_STEP == 0
    assert n_pages // PAGES_PER_BLOCK >= MOBA_TOPK

    w_hi, w_lo = _split_bf16(w_in[0])
    w_lo = w_lo[:, :2 * COL_BLOCK]
    gqk = jnp.stack([jnp.tile(g_qnorm[0], A_HEADS), jnp.tile(g_knorm[0], A_HEADS)]).reshape(2, 1, A_WIDTH)
    head_of = np.arange(A_WIDTH) // A_HEAD_DIM
    bd = jnp.asarray(head_of[:, None] == head_of[None, :], BF16)
    head_ind = jnp.asarray(head_of[:, None] == np.arange(LANES)[None, :], F32)
    wa, wr, wo, wpq = (w[0].astype(BF16) for w in (w_a_up, w_r_up, w_out, w_pq))
    sk2 = jnp.zeros((P_HEADS, 2 * P_NKEYS, P_KEY_DIM), F32)
    sk2 = sk2.at[:, :P_NKEYS, :P_HALF].set(sub_keys[0, :, 0]).at[:, P_NKEYS:, P_HALF:].set(sub_keys[0, :, 1])
    sk_hi, sk_lo = _split_bf16(sk2)
    u = expert_u[0].astype(BF16)
    vt = expert_v[0].T.astype(BF16)
    g1, g2, go = g_norm1[0].reshape(1, d), g_norm2[0].reshape(1, d), g_onorm[0].reshape(1, R_VAL_DIM)

    n_c = bp + bs
    c_all = jnp.concatenate([c_prompt, c_sample, jnp.zeros((-n_c % 8, d), F32)], axis=0)
    mod = _ada(c_all, w_ada[0], b_ada[0])
    mod_p = mod[:bp].reshape(bp, 6, 1, d)
    mod_s = mod[bp:n_c].reshape(1, bs, 6, d)
    sh1p, sc1p, gt1p, sh2p, sc2p, gt2p = (mod_p[:, i] for i in range(6))
    sh1s, sc1s, gt1s, sh2s, sc2s, gt2s = (mod_s[:, :, i] for i in range(6))
    bias_t = _bias_tiles(rel_bias)

    xp = x_prompt.reshape(bp * t, d)
    zp = _inproj(xp, sc1p, sh1p, g1, w_hi, w_lo, gqk, bd, _row_tile(t, 1024))
    nb = t // MOBA_BLOCK
    vt_p = zp[:, 2 * A_WIDTH:3 * A_WIDTH].reshape(bp, nb, MOBA_BLOCK, A_HEADS // MOBA_HEADS_PER_STEP, MOBA_GROUP_WIDTH)
    vt_p = vt_p.transpose(0, 3, 1, 4, 2)
    oa_p = _moba_prompt(zp.reshape(bp, nb, MOBA_BLOCK, IN_WIDTH), vt_p, bias_t).reshape(bp * t, A_WIDTH)
    or_p, s_prompt = _hgrn_prompt(zp, lb_param, bp)
    x1p = _merge(oa_p, or_p, zp, xp, gt1p, go, wa, wr, wo, _row_tile(t, 512))
    y_prompt = _peer(x1p, sc2p, sh2p, gt2p, g2, wpq, sk_hi, sk_lo, u, vt, _row_tile(t, 1024))

    xs = x_sample.reshape(bs, d)
    zs = _inproj(xs, sc1s, sh1s, g1, w_hi, w_lo, gqk, bd, bs)
    z3 = zs.reshape(bs, 1, IN_WIDTH)
    pt_flat = page_table.reshape(-1)
    km = _cache_means(cache_k[0], pt_flat, bs, n_pages).reshape(bs, -1, A_WIDTH)
    sel = _select_blocks(z3, km, head_ind)[:, :MOBA_TOPK, :A_HEADS].transpose(0, 2, 1).reshape(-1)
    new3 = lambda lo: zs[:, lo:lo + A_WIDTH].reshape(bs, A_HEADS, A_HEAD_DIM)
    oa_s = _moba_sample(new3(0), new3(A_WIDTH), new3(2 * A_WIDTH), cache_k[0], cache_v[0], sel, pt_flat, rel_bias,
                        n_pages).reshape(bs, A_WIDTH)
    or_s, s_sample = _hgrn_sample(z3, lb_param, state_hgrn[0])
    x1s = _merge(oa_s, or_s.reshape(bs, R_WIDTH), zs, xs, gt1s, go, wa, wr, wo, bs)
    y_sample = _peer(x1s, sc2s, sh2s, gt2s, g2, wpq, sk_hi, sk_lo, u, vt, bs)

    heads = lambda z2, lo, lead: z2[:, lo:lo + A_WIDTH].reshape(1, *lead, A_HEADS, A_HEAD_DIM)
    return (y_prompt.reshape(bp, t, d), y_sample.reshape(bs, 1, d),
            heads(zp, A_WIDTH, (bp, t)), heads(zp, 2 * A_WIDTH, (bp, t)), s_prompt[None],
            heads(zs, A_WIDTH, (bs, 1)), heads(zs, 2 * A_WIDTH, (bs, 1)), s_sample[None])
```

```python
import functools
import math

import numpy as np
import jax
import jax.numpy as jnp
from jax import lax
from jax.experimental import pallas as pl
from jax.experimental.pallas import tpu as pltpu

F32 = jnp.float32
BF16 = jnp.bfloat16
HIGHEST = lax.Precision.HIGHEST

D_MODEL = 1024
EPS = 1e-6
PAGE_SIZE = 128
A_HEADS = 8
A_HEAD_DIM = 64
A_WIDTH = A_HEADS * A_HEAD_DIM
A_SCALE = A_HEAD_DIM ** -0.5
MOBA_BLOCK = 256
MOBA_TOPK = 3
NUM_BUCKETS = 32
MAX_DISTANCE = 128
R_HEADS = 4
R_KEY_DIM = 128
R_VAL_DIM = 128
R_WIDTH = R_HEADS * R_KEY_DIM
R_CHUNK = 32
P_HEADS = 8
P_NKEYS = 128
P_KEY_DIM = 128
P_HALF = P_KEY_DIM // 2
P_TOPK = 16
IN_WIDTH = 3 * A_WIDTH + 4 * R_WIDTH + 2 * D_MODEL
COL_BLOCK = 512
N_COL_BLOCKS = IN_WIDTH // COL_BLOCK
NEG = -1e30
VMEM_LIMIT = 56 * 1024 * 1024


def _dot(a, b, precision=None):
    return jnp.dot(a, b, preferred_element_type=F32, precision=precision)


def _dot_nt(a, b, precision=None):
    return lax.dot_general(a, b, (((1,), (1,)), ((), ())), preferred_element_type=F32, precision=precision)


def _silu(x):
    return x * jax.nn.sigmoid(x)


def _params(sem, vmem=VMEM_LIMIT):
    return pltpu.CompilerParams(dimension_semantics=sem, vmem_limit_bytes=vmem)


def _ada_kernel(c_ref, w_ref, b_ref, o_ref):
    o_ref[...] = _dot(_silu(c_ref[...]), w_ref[...], HIGHEST) + b_ref[...]


def _ada(c, w_ada, b_ada):
    n, d = c.shape
    width = w_ada.shape[1]
    tn = 768
    return pl.pallas_call(
        _ada_kernel,
        grid=(width // tn,),
        in_specs=[pl.BlockSpec((n, d), lambda j: (0, 0)),
                  pl.BlockSpec((d, tn), lambda j: (0, j)),
                  pl.BlockSpec((1, tn), lambda j: (0, j))],
        out_specs=pl.BlockSpec((n, tn), lambda j: (0, j)),
        out_shape=jax.ShapeDtypeStruct((n, width), F32),
        compiler_params=_params(("arbitrary",)),
        name="ada_mod",
    )(c, w_ada, b_ada.reshape(1, width))


def _bucket_starts():
    max_exact = NUM_BUCKETS // 2
    n = np.arange(0, 4 * MAX_DISTANCE)
    nf = np.maximum(n, 1).astype(np.float32)
    large = max_exact + (np.log(nf / np.float32(max_exact)) / np.float32(math.log(MAX_DISTANCE / max_exact))
                         * np.float32(NUM_BUCKETS - max_exact)).astype(np.int32)
    bucket = np.where(n < max_exact, n, np.minimum(large, NUM_BUCKETS - 1))
    assert (np.diff(bucket) >= 0).all() and bucket[-1] == NUM_BUCKETS - 1
    return [int(np.argmax(bucket >= b)) for b in range(NUM_BUCKETS)]


_BUCKET_START = _bucket_starts()


def _bias_of_distance(dist, rb_ref, h):
    val = jnp.full(dist.shape, rb_ref[NUM_BUCKETS - 1, h], F32)
    for b in range(NUM_BUCKETS - 2, -1, -1):
        val = jnp.where(dist < _BUCKET_START[b + 1], rb_ref[b, h], val)
    return val


def _bias_kernel(rb_ref, o_ref):
    h = pl.program_id(0)
    d = pl.program_id(1)
    k = lax.broadcasted_iota(jnp.int32, (MOBA_BLOCK, MOBA_BLOCK), 0)
    q = lax.broadcasted_iota(jnp.int32, (MOBA_BLOCK, MOBA_BLOCK), 1)
    dist = d * MOBA_BLOCK + q - k
    val = _bias_of_distance(dist, rb_ref, h)
    o_ref[...] = jnp.where(dist >= 0, val, NEG)


def _bias_tiles(rel_bias):
    return pl.pallas_call(
        _bias_kernel,
        grid=(A_HEADS, 3),
        in_specs=[pl.BlockSpec(memory_space=pltpu.SMEM)],
        out_specs=pl.BlockSpec((None, None, MOBA_BLOCK, MOBA_BLOCK), lambda h, d: (h, d, 0, 0)),
        out_shape=jax.ShapeDtypeStruct((A_HEADS, 3, MOBA_BLOCK, MOBA_BLOCK), F32),
        compiler_params=_params(("arbitrary", "arbitrary")),
        name="moba_bias_tiles",
    )(rel_bias)


def _rms_mod(x, g, sc, sh):
    ms = jnp.mean(x * x, axis=-1, keepdims=True)
    return (x * lax.rsqrt(ms + EPS) * g) * (1.0 + sc) + sh


def _split_bf16(x):
    hi = x.astype(BF16)
    return hi, (x - hi.astype(F32)).astype(BF16)


def _inproj_kernel(x_ref, sc_ref, sh_ref, g_ref, whi_ref, wlo_ref, gqk_ref, bd_ref, o_ref, hhi_ref, hlo_ref):
    j = pl.program_id(1)

    @pl.when(j == 0)
    def _():
        hhi, hlo = _split_bf16(_rms_mod(x_ref[...], g_ref[...], sc_ref[...], sh_ref[...]))
        hhi_ref[...] = hhi
        hlo_ref[...] = hlo

    @pl.when(j < 2)
    def _():
        hhi = hhi_ref[...]
        whi = whi_ref[...]
        z = _dot(hhi, whi) + (_dot(hhi, wlo_ref[...]) + _dot(hlo_ref[...], whi))
        zhi, zlo = _split_bf16(z * z)
        ssq = _dot(zhi, bd_ref[...]) + _dot(zlo, bd_ref[...])
        o_ref[...] = z * lax.rsqrt(ssq * (1.0 / A_HEAD_DIM) + EPS) * gqk_ref[...]

    @pl.when(j >= 2)
    def _():
        o_ref[...] = _dot(hhi_ref[...], whi_ref[...])


def _inproj(x2d, sc, sh, g, whi, wlo, gqk, bd, tm):
    n, d = x2d.shape
    groups, r, _ = sc.shape
    tiles_per_group = n // tm // groups
    mod_spec = pl.BlockSpec((None, r, d), lambda i, j: (i // tiles_per_group, 0, 0))
    return pl.pallas_call(
        _inproj_kernel,
        grid=(n // tm, N_COL_BLOCKS),
        in_specs=[pl.BlockSpec((tm, d), lambda i, j: (i, 0)),
                  mod_spec, mod_spec,
                  pl.BlockSpec((1, d), lambda i, j: (0, 0)),
                  pl.BlockSpec((d, COL_BLOCK), lambda i, j: (0, j)),
                  pl.BlockSpec((d, COL_BLOCK), lambda i, j: (0, jnp.minimum(j, 1))),
                  pl.BlockSpec((None, 1, COL_BLOCK), lambda i, j: (jnp.minimum(j, 1), 0, 0)),
                  pl.BlockSpec((COL_BLOCK, COL_BLOCK), lambda i, j: (0, 0))],
        out_specs=pl.BlockSpec((tm, COL_BLOCK), lambda i, j: (i, j)),
        out_shape=jax.ShapeDtypeStruct((n, IN_WIDTH), F32),
        scratch_shapes=[pltpu.VMEM((tm, d), BF16), pltpu.VMEM((tm, d), BF16)],
        compiler_params=_params(("arbitrary", "arbitrary")),
        name="in_proj",
    )(x2d, sc, sh, g, whi, wlo, gqk, bd)


def _third_largest(s):
    m = jnp.max(s, axis=0, keepdims=True)
    for _ in range(MOBA_TOPK - 1):
        s = jnp.where(s >= m, NEG, s)
        m = jnp.max(s, axis=0, keepdims=True)
    return m


MOBA_HEADS_PER_STEP = 4
MOBA_GROUP_WIDTH = MOBA_HEADS_PER_STEP * A_HEAD_DIM


def _moba_prompt_kernel(q_ref, k_ref, vt_ref, bias_ref, o_ref, km_ref, selb_ref, s_ref, acc_ref):
    qi = pl.program_id(2)
    nb, tk, _ = k_ref.shape
    tq = q_ref.shape[0]
    nh = MOBA_HEADS_PER_STEP

    @pl.when(qi == 0)
    def _():
        km_ref[...] = jnp.mean(k_ref[...], axis=1)

    q = q_ref[...]
    lane = lax.broadcasted_iota(jnp.int32, (1, MOBA_GROUP_WIDTH), 1)
    head_lanes = [(lane // A_HEAD_DIM) == h for h in range(nh)]
    blk = lax.broadcasted_iota(jnp.int32, (nb, tq), 0)
    for h in range(nh):
        s = _dot_nt(jnp.where(head_lanes[h], km_ref[...], 0.0), q, HIGHEST)
        s = jnp.where(blk < qi, s, NEG)
        sel = (blk < qi) & (s >= _third_largest(s))
        selb_ref[h] = jnp.where(sel | (blk == qi), 0.0, NEG)

    qs = (q * A_SCALE).astype(BF16)

    n_blocks = qi + 1

    def by_pairs(step, init):
        c = lax.fori_loop(0, n_blocks // 2, lambda i, c: step(2 * i + 1, step(2 * i, c)), init)
        return lax.cond(n_blocks % 2 == 1, lambda c: step(n_blocks - 1, c), lambda c: c, c)

    def logits_of(j, m):
        kb = k_ref[j]
        dslot = jnp.minimum(qi - j, 2)
        out = []
        for h in range(nh):
            kh = jnp.where(head_lanes[h], kb, 0.0).astype(BF16)
            lg = _dot_nt(kh, qs) + bias_ref[h, dslot] + selb_ref[h, pl.ds(j, 1), :]
            s_ref[j, h] = lg
            out.append(jnp.maximum(m[h], jnp.max(lg, axis=0, keepdims=True)))
        return tuple(out)

    m = by_pairs(logits_of, tuple(jnp.full((1, tq), NEG, F32) for _ in range(nh)))

    acc_ref[...] = jnp.zeros_like(acc_ref)

    def weigh(j, l):
        vtb = vt_ref[j].astype(BF16)
        out = []
        for h in range(nh):
            p = jnp.exp(s_ref[j, h] - m[h])
            acc_ref[h] += _dot(vtb[h * A_HEAD_DIM:(h + 1) * A_HEAD_DIM, :], p.astype(BF16))
            out.append(l[h] + jnp.sum(p, axis=0, keepdims=True))
        return tuple(out)

    l = by_pairs(weigh, tuple(jnp.zeros((1, tq), F32) for _ in range(nh)))
    ot = jnp.concatenate([acc_ref[h] / l[h] for h in range(nh)], axis=0)
    o_ref[...] = ot.T


def _moba_prompt(z4, vt, bias_t):
    b, nb, tk, _ = z4.shape
    groups = A_HEADS // MOBA_HEADS_PER_STEP
    w = MOBA_GROUP_WIDTH
    return pl.pallas_call(
        _moba_prompt_kernel,
        grid=(b, groups, nb),
        in_specs=[pl.BlockSpec((None, None, tk, w), lambda bi, g, qi: (bi, qi, 0, g)),
                  pl.BlockSpec((None, nb, tk, w), lambda bi, g, qi: (bi, 0, 0, groups + g)),
                  pl.BlockSpec((None, None, nb, w, tk), lambda bi, g, qi: (bi, g, 0, 0, 0)),
                  pl.BlockSpec((MOBA_HEADS_PER_STEP, 3, tk, tk), lambda bi, g, qi: (g, 0, 0, 0))],
        out_specs=pl.BlockSpec((None, None, tk, w), lambda bi, g, qi: (bi, qi, 0, g)),
        out_shape=jax.ShapeDtypeStruct((b, nb, tk, A_WIDTH), F32),
        scratch_shapes=[pltpu.VMEM((nb, w), F32), pltpu.VMEM((MOBA_HEADS_PER_STEP, nb, tk), F32),
                        pltpu.VMEM((nb, MOBA_HEADS_PER_STEP, tk, tk), F32),
                        pltpu.VMEM((MOBA_HEADS_PER_STEP, A_HEAD_DIM, tk), F32)],
        compiler_params=_params(("arbitrary", "arbitrary", "arbitrary")),
        name="moba_prompt",
    )(z4, z4, vt, bias_t)


def _hgrn_gates(qr, fr, lb_param):
    e = jnp.exp(lb_param - jnp.max(lb_param, axis=0, keepdims=True))
    lb = e[0:1, :] / jnp.sum(e, axis=0, keepdims=True)
    f = lb + (1.0 - lb) * jax.nn.sigmoid(fr)
    return _silu(qr), f


def _hgrn_prompt_kernel(qr_ref, fr_ref, ir_ref, lb_ref, o_ref, s_ref, st_ref):
    tb = pl.program_id(1)
    t = qr_ref.shape[0]
    c = R_CHUNK
    row = lax.broadcasted_iota(jnp.int32, (c, c), 0)
    col = lax.broadcasted_iota(jnp.int32, (c, c), 1)
    tril = row >= col
    tril_f = tril.astype(F32)
    lb = lb_ref[...]

    @pl.when(tb == 0)
    def _():
        st_ref[...] = jnp.zeros_like(st_ref)

    def chunk(ci, _):
        rows = pl.ds(pl.multiple_of(ci * c, c), c)
        q, f = _hgrn_gates(qr_ref[rows, :], fr_ref[rows, :], lb)
        k = 1.0 - f
        v = ir_ref[rows, :]
        b = _dot(tril_f, jnp.log(f), HIGHEST)
        bl = b[c - 1:c, :]
        qe = (q * jnp.exp(b)).astype(BF16)
        ke = (k * jnp.exp(-b)).astype(BF16)
        kd = (k * jnp.exp(bl - b)).astype(BF16)
        decay = jnp.exp(bl)
        outs = []
        for h in range(R_HEADS):
            sl = slice(h * R_KEY_DIM, (h + 1) * R_KEY_DIM)
            st = st_ref[h]
            vh = v[:, sl]
            att = jnp.where(tril, _dot_nt(qe[:, sl], ke[:, sl]), 0.0)
            outs.append(_dot_nt(qe[:, sl], st.astype(BF16)) + _dot(att.astype(BF16), vh.astype(BF16)))
            st_ref[h] = st * decay[:, sl] + _dot(vh.T.astype(BF16), kd[:, sl])
        o_ref[rows, :] = jnp.concatenate(outs, axis=-1)
        return 0

    lax.fori_loop(0, t // c, chunk, 0, unroll=2)

    @pl.when(tb == pl.num_programs(1) - 1)
    def _():
        for h in range(R_HEADS):
            s_ref[h] = st_ref[h].T


def _hgrn_prompt(z2d, lb_param, batch):
    n = z2d.shape[0]
    t = n // batch
    tb = _row_tile(t, 1024)
    col0 = 3 * A_WIDTH // R_WIDTH
    spec = lambda off: pl.BlockSpec((tb, R_WIDTH), lambda bi, ti: (bi * (t // tb) + ti, col0 + off))
    state = pl.BlockSpec((None, R_HEADS, R_KEY_DIM, R_VAL_DIM), lambda bi, ti: (bi, 0, 0, 0))
    return pl.pallas_call(
        _hgrn_prompt_kernel,
        grid=(batch, t // tb),
        in_specs=[spec(0), spec(1), spec(2), pl.BlockSpec(lb_param.shape, lambda bi, ti: (0, 0))],
        out_specs=[pl.BlockSpec((tb, R_WIDTH), lambda bi, ti: (bi * (t // tb) + ti, 0)), state],
        out_shape=[jax.ShapeDtypeStruct((n, R_WIDTH), F32),
                   jax.ShapeDtypeStruct((batch, R_HEADS, R_KEY_DIM, R_VAL_DIM), F32)],
        scratch_shapes=[pltpu.VMEM((R_HEADS, R_VAL_DIM, R_KEY_DIM), F32)],
        compiler_params=_params(("arbitrary", "arbitrary")),
        name="hgrn_prompt",
    )(z2d, z2d, z2d, lb_param)


def _merge_kernel(oa_ref, or_ref, gr_ref, ga0_ref, ga1_ref, gb0_ref, gb1_ref, x_ref, gt_ref, go_ref,
                  wa_ref, wr_ref, wo_ref, o_ref):
    o_r = or_ref[...]
    go = go_ref[...]
    heads = []
    for h in range(R_HEADS):
        v = o_r[:, h * R_VAL_DIM:(h + 1) * R_VAL_DIM]
        heads.append(v * lax.rsqrt(jnp.mean(v * v, axis=-1, keepdims=True) + EPS) * go)
    o_r = jnp.concatenate(heads, axis=-1) * _silu(gr_ref[...])
    ga = jnp.concatenate([ga0_ref[...], ga1_ref[...]], axis=-1)
    gb = jnp.concatenate([gb0_ref[...], gb1_ref[...]], axis=-1)
    merged = (jax.nn.sigmoid(ga) * _dot(oa_ref[...].astype(BF16), wa_ref[...])
              + jax.nn.sigmoid(gb) * _dot(o_r.astype(BF16), wr_ref[...]))
    o_ref[...] = x_ref[...] + gt_ref[...] * _dot(merged.astype(BF16), wo_ref[...])


def _merge(o_a, o_r, z2d, x2d, gt, g_o, wa, wr, wo, tm):
    n, d = x2d.shape
    groups, r, _ = gt.shape
    tiles_per_group = n // tm // groups
    zcol = lambda cb: pl.BlockSpec((tm, COL_BLOCK), lambda i: (i, cb))
    full = lambda a: pl.BlockSpec(a.shape, lambda i: (0,) * a.ndim)
    return pl.pallas_call(
        _merge_kernel,
        grid=(n // tm,),
        in_specs=[pl.BlockSpec((tm, A_WIDTH), lambda i: (i, 0)),
                  pl.BlockSpec((tm, R_WIDTH), lambda i: (i, 0)),
                  zcol(6), zcol(7), zcol(8), zcol(9), zcol(10),
                  pl.BlockSpec((tm, d), lambda i: (i, 0)),
                  pl.BlockSpec((None, r, d), lambda i: (i // tiles_per_group, 0, 0)),
                  full(g_o), full(wa), full(wr), full(wo)],
        out_specs=pl.BlockSpec((tm, d), lambda i: (i, 0)),
        out_shape=jax.ShapeDtypeStruct((n, d), F32),
        compiler_params=_params(("arbitrary",)),
        name="merge_out_proj",
    )(o_a, o_r, z2d, z2d, z2d, z2d, z2d, x2d, gt, g_o, wa, wr, wo)


_STAIR = [P_TOPK // (a + 1) for a in range(P_TOPK)]
LANES = 128
ROWS_PER_BLOCK = 4
E_BLOCK = ROWS_PER_BLOCK * P_NKEYS


def _top_rows(s, n):
    rows = []
    for r in range(n):
        m = jnp.max(s, axis=0, keepdims=True)
        rows.append(m)
        if r + 1 < n:
            s = jnp.where(s >= m, NEG, s)
    return rows


def _stack_rows(rows):
    n = len(rows)
    idx = lax.broadcasted_iota(jnp.int32, (n, LANES), 0)
    out = jnp.zeros((n, LANES), F32)
    for r, row in enumerate(rows):
        out = jnp.where(idx == r, row, out)
    return out


def _top_rows_ranked(s, n):
    rows = []
    rank = jnp.full(s.shape, float(n), F32)
    for r in range(n):
        m = jnp.max(s, axis=0, keepdims=True)
        rows.append(m)
        hit = s >= m
        rank = jnp.where(hit, float(r), rank)
        s = jnp.where(hit, NEG, s)
    return rows, rank


def _route(s0, s1):
    row8 = lax.broadcasted_iota(jnp.int32, (8, LANES), 0)
    v0 = _top_rows(s0, P_TOPK)
    v1, rank1 = _top_rows_ranked(s1, P_TOPK)
    v1_all = _stack_rows(v1)
    cands = [v0[0] + v1_all, v0[1] + v1_all[:8]]
    cands += [jnp.where(row8 < _STAIR[a], v0[a] + v1_all[:8], NEG) for a in range(2, P_TOPK)]
    cand = jnp.concatenate(cands, axis=0)
    tau = _top_rows(cand, P_TOPK)[-1]
    top = v0[0] + v1[0]
    z = jnp.sum(jnp.where(cand >= tau, jnp.exp(cand - top), 0.0), axis=0, keepdims=True)
    n = jnp.zeros(s0.shape, F32)
    for a in range(P_TOPK):
        n_a = jnp.sum(jnp.where(cands[a] >= tau, 1.0, 0.0), axis=0, keepdims=True)
        n = jnp.where(s0 == v0[a], n_a, n)
    return n, jnp.exp(s0 - v0[0]) / z, rank1.astype(BF16), jnp.exp(s1 - v1[0]).astype(BF16)


def _peer_kernel(x_ref, sc_ref, sh_ref, gt_ref, g_ref, wpq_ref, skh_ref, skl_ref, u0_ref, ua_ref, ub_ref,
                 vta_ref, vtb_ref, o_ref, h2_ref, n_ref, e0_ref, r1_ref, e1_ref, a0_ref, a1_ref, g0_ref, g1_ref,
                 acc_ref):
    s = pl.program_id(1)
    tm = x_ref.shape[0]
    nc = tm // LANES
    group = min(nc, 2)

    @pl.when(s == 0)
    def _():
        h2 = _rms_mod(x_ref[...], g_ref[...], sc_ref[...], sh_ref[...]).astype(BF16)
        h2_ref[...] = h2
        q = _dot(h2, wpq_ref[...])
        for h in range(P_HEADS):
            qh, ql = _split_bf16(q[:, h * P_KEY_DIM:(h + 1) * P_KEY_DIM])
            skh = skh_ref[h]
            st = _dot_nt(skh, qh) + (_dot_nt(skh, ql) + _dot_nt(skl_ref[h], qh))
            for c in range(nc):
                n_ref[h * nc + c] = st[:P_NKEYS, c * LANES:(c + 1) * LANES]
                e0_ref[h * nc + c] = st[P_NKEYS:, c * LANES:(c + 1) * LANES]

        def route(pair, _):
            for idx in (2 * pair, 2 * pair + 1):
                n, e0, r1, e1 = _route(n_ref[idx], e0_ref[idx])
                n_ref[idx] = n
                e0_ref[idx] = 0.5 * e0
                r1_ref[idx] = r1.reshape(P_NKEYS // 16, 16, LANES)
                e1_ref[idx] = e1.reshape(P_NKEYS // 16, 16, LANES)
            return 0

        lax.fori_loop(0, P_HEADS * nc // 2, route, 0)
        acc_ref[...] = jnp.zeros_like(acc_ref)
        a0_ref[...] = _dot_nt(u0_ref[...], h2)

    def gate(blk, a_ref, g_ref_, c):
        w = [jnp.zeros((P_NKEYS // 16, 16, LANES), BF16) for _ in range(ROWS_PER_BLOCK)]
        for h in range(P_HEADS):
            idx = h * nc + c
            r1 = r1_ref[idx]
            e1 = e1_ref[idx]
            for ii in range(ROWS_PER_BLOCK):
                i = ROWS_PER_BLOCK * blk + ii
                n_row = jnp.broadcast_to(n_ref[idx, pl.ds(i, 1), :], (16, LANES)).astype(BF16)
                e0_row = jnp.broadcast_to(e0_ref[idx, pl.ds(i, 1), :], (16, LANES)).astype(BF16)
                w[ii] = w[ii] + jnp.where(r1 < n_row[None], e1, jnp.zeros((), BF16)) * e0_row[None]
        for ii in range(ROWS_PER_BLOCK):
            a = a_ref[ii * P_NKEYS:(ii + 1) * P_NKEYS, c * LANES:(c + 1) * LANES]
            act = (a * (1.0 + lax.erf(a * (1.0 / math.sqrt(2.0))))).astype(BF16)
            g_ref_[ii * P_NKEYS:(ii + 1) * P_NKEYS, c * LANES:(c + 1) * LANES] = act * w[ii].reshape(P_NKEYS, LANES)

    def half_step(blk, a_src, a_dst, u_next, g_dst, vt_cur):
        for cg in range(nc // group):
            cols = slice(cg * group * LANES, (cg + 1) * group * LANES)
            a_dst[:, cols] = _dot_nt(u_next[...], h2_ref[cols, :])
            for c in range(cg * group, (cg + 1) * group):
                gate(blk, a_src, g_dst, c)
            acc_ref[:, cols] += _dot(vt_cur[...], g_dst[:, cols])

    half_step(2 * s, a0_ref, a1_ref, ua_ref, g0_ref, vta_ref)
    half_step(2 * s + 1, a1_ref, a0_ref, ub_ref, g1_ref, vtb_ref)

    @pl.when(s == pl.num_programs(1) - 1)
    def _():
        o_ref[...] = x_ref[...] + gt_ref[...] * acc_ref[...].T


def _peer(x2d, sc, sh, gt, g, wpq, skh, skl, u, vt, tm):
    n, d = x2d.shape
    groups, r, _ = sc.shape
    tiles_per_group = n // tm // groups
    nc = tm // LANES
    n_blk = u.shape[0] // E_BLOCK
    mod_spec = pl.BlockSpec((None, r, d), lambda i, s: (i // tiles_per_group, 0, 0))
    route32 = pltpu.VMEM((P_HEADS * nc, P_NKEYS, LANES), F32)
    route16 = pltpu.VMEM((P_HEADS * nc, P_NKEYS // 16, 16, LANES), BF16)
    pre = pltpu.VMEM((E_BLOCK, tm), F32)
    gated = pltpu.VMEM((E_BLOCK, tm), BF16)
    single = pl.Buffered(1)
    return pl.pallas_call(
        _peer_kernel,
        grid=(n // tm, n_blk // 2),
        in_specs=[pl.BlockSpec((tm, d), lambda i, s: (i, 0), pipeline_mode=single),
                  mod_spec, mod_spec, mod_spec,
                  pl.BlockSpec((1, d), lambda i, s: (0, 0)),
                  pl.BlockSpec(wpq.shape, lambda i, s: (0, 0), pipeline_mode=single),
                  pl.BlockSpec(skh.shape, lambda i, s: (0, 0, 0), pipeline_mode=single),
                  pl.BlockSpec(skl.shape, lambda i, s: (0, 0, 0), pipeline_mode=single),
                  pl.BlockSpec((E_BLOCK, d), lambda i, s: (0, 0), pipeline_mode=single),
                  pl.BlockSpec((E_BLOCK, d), lambda i, s: (2 * s + 1, 0)),
                  pl.BlockSpec((E_BLOCK, d), lambda i, s: (jnp.minimum(2 * s + 2, n_blk - 1), 0)),
                  pl.BlockSpec((d, E_BLOCK), lambda i, s: (0, 2 * s)),
                  pl.BlockSpec((d, E_BLOCK), lambda i, s: (0, 2 * s + 1))],
        out_specs=pl.BlockSpec((tm, d), lambda i, s: (i, 0), pipeline_mode=single),
        out_shape=jax.ShapeDtypeStruct((n, d), F32),
        scratch_shapes=[pltpu.VMEM((tm, d), BF16), route32, route32, route16, route16, pre, pre, gated, gated,
                        pltpu.VMEM((d, tm), F32)],
        compiler_params=_params(("arbitrary", "arbitrary")),
        name="peer",
    )(x2d, sc, sh, gt, g, wpq, skh, skl, u, u, u, vt, vt)


PAGES_PER_STEP = 16
PAGES_PER_BLOCK = MOBA_BLOCK // PAGE_SIZE
TILES = MOBA_TOPK * PAGES_PER_BLOCK
KEYS = TILES * PAGE_SIZE


def _cache_select_kernel(pt_ref, q_ref, *refs):
    pages, o_ref, s_ref = refs[:PAGES_PER_STEP], refs[PAGES_PER_STEP], refs[PAGES_PER_STEP + 1]
    g = pl.program_id(1)
    blocks_per_step = PAGES_PER_STEP // PAGES_PER_BLOCK
    row = lax.broadcasted_iota(jnp.int32, (A_HEADS, LANES), 0)
    lane = lax.broadcasted_iota(jnp.int32, (A_HEADS, LANES), 1)

    @pl.when(g == 0)
    def _():
        s_ref[...] = jnp.full(s_ref.shape, NEG, F32)

    s = s_ref[...]
    for blk in range(blocks_per_step):
        per_token = jnp.zeros((A_HEADS, PAGE_SIZE), F32)
        for h in range(A_HEADS):
            qk = jnp.zeros((1, PAGE_SIZE), F32)
            for p in range(PAGES_PER_BLOCK):
                qk = qk + jnp.sum(pages[blk * PAGES_PER_BLOCK + p][h] * q_ref[h], axis=0, keepdims=True)
            per_token = jnp.where(row == h, qk, per_token)
        score = jnp.sum(per_token, axis=-1, keepdims=True) * (1.0 / MOBA_BLOCK)
        s = jnp.where(lane == g * blocks_per_step + blk, score, s)
    s_ref[...] = s

    @pl.when(g == pl.num_programs(1) - 1)
    def _():
        sc = s
        out = jnp.zeros((A_HEADS, LANES), jnp.int32)
        for r in range(MOBA_TOPK):
            m = jnp.max(sc, axis=-1, keepdims=True)
            idx = jnp.min(jnp.where(sc >= m, lane, LANES), axis=-1, keepdims=True)
            out = jnp.where(lane == r, idx, out)
            sc = jnp.where(lane == idx, NEG, sc)
        o_ref[...] = out


def _cache_select(q_cols, cache_t, pt_flat, n_pages):
    batch = q_cols.shape[0]
    _, nh, dh, _ = cache_t.shape
    steps = n_pages // PAGES_PER_STEP
    page = lambda p: pl.BlockSpec((None, nh, dh, PAGE_SIZE),
                                  lambda b, g, pt: (pt[b * n_pages + g * PAGES_PER_STEP + p], 0, 0, 0))
    return pl.pallas_call(
        _cache_select_kernel,
        grid_spec=pltpu.PrefetchScalarGridSpec(
            num_scalar_prefetch=1,
            grid=(batch, steps),
            in_specs=[pl.BlockSpec((None, nh, dh, 1), lambda b, g, pt: (b, 0, 0, 0))]
                     + [page(p) for p in range(PAGES_PER_STEP)],
            out_specs=pl.BlockSpec((None, nh, LANES), lambda b, g, pt: (b, 0, 0)),
            scratch_shapes=[pltpu.VMEM((nh, LANES), F32)]),
        out_shape=jax.ShapeDtypeStruct((batch, nh, LANES), jnp.int32),
        compiler_params=_params(("arbitrary", "arbitrary")),
        name="cache_select_blocks",
    )(pt_flat, q_cols, *([cache_t] * PAGES_PER_STEP))


def _moba_sample_kernel(past_len, n_pages, sel_ref, pt_ref, rb_ref, q_ref, kn_ref, vn_ref, ck_ref, cv_ref, o_ref,
                        kbuf, vbuf, sem):
    b = pl.program_id(0)
    nb = pl.num_programs(0)

    def copies(bb, slot):
        out = []
        for h in range(A_HEADS):
            for r in range(MOBA_TOPK):
                blk = sel_ref[(bb * A_HEADS + h) * MOBA_TOPK + r]
                for p in range(PAGES_PER_BLOCK):
                    page = pt_ref[bb * n_pages + blk * PAGES_PER_BLOCK + p]
                    cols = pl.ds((r * PAGES_PER_BLOCK + p) * PAGE_SIZE, PAGE_SIZE)
                    out.append(pltpu.make_async_copy(ck_ref.at[page, h], kbuf.at[slot, h, :, cols], sem.at[slot]))
                    out.append(pltpu.make_async_copy(cv_ref.at[page, h], vbuf.at[slot, h, :, cols], sem.at[slot]))
        return out

    @pl.when(b == 0)
    def _():
        for c in copies(0, 0):
            c.start()

    @pl.when(b + 1 < nb)
    def _():
        for c in copies(b + 1, (b + 1) % 2):
            c.start()

    slot = b % 2
    for c in copies(b, slot):
        c.wait()

    row8 = lax.broadcasted_iota(jnp.int32, (8, A_HEAD_DIM), 0)
    row8k = lax.broadcasted_iota(jnp.int32, (8, KEYS), 0)
    lane = lax.broadcasted_iota(jnp.int32, (1, KEYS), 1)
    for h in range(A_HEADS):
        qh = q_ref[h:h + 1, :] * A_SCALE
        kn = kn_ref[h:h + 1, :]
        vn = vn_ref[h:h + 1, :]
        q8 = jnp.where(row8 == 0, qh, 0.0).astype(BF16)
        start = [past_len - sel_ref[(b * A_HEADS + h) * MOBA_TOPK + r] * MOBA_BLOCK for r in range(MOBA_TOPK)]
        which = lane // MOBA_BLOCK
        dist = jnp.where(which == 0, start[0], jnp.where(which == 1, start[1], start[2])) - lane % MOBA_BLOCK
        logits = _dot(q8, kbuf[slot, h].astype(BF16))[0:1, :] + _bias_of_distance(dist, rb_ref, h)
        own = jnp.sum(qh * kn, axis=-1, keepdims=True) + rb_ref[0, h]
        m = jnp.maximum(own, jnp.max(logits, axis=-1, keepdims=True))
        p_own = jnp.exp(own - m)
        p = jnp.exp(logits - m)
        l = p_own + jnp.sum(p, axis=-1, keepdims=True)
        p8 = jnp.where(row8k == 0, p, 0.0).astype(BF16)
        o_ref[h:h + 1, :] = (_dot_nt(p8, vbuf[slot, h].astype(BF16))[0:1, :] + p_own * vn) / l


def _moba_sample(q3, kn3, vn3, cache_kt, cache_vt, sel_flat, pt_flat, rel_bias, n_pages):
    batch = q3.shape[0]
    row = pl.BlockSpec((None, A_HEADS, A_HEAD_DIM), lambda b, sel, pt: (b, 0, 0))
    buf = pltpu.VMEM((2, A_HEADS, A_HEAD_DIM, KEYS), F32)
    return pl.pallas_call(
        functools.partial(_moba_sample_kernel, n_pages * PAGE_SIZE, n_pages),
        grid_spec=pltpu.PrefetchScalarGridSpec(
            num_scalar_prefetch=2,
            grid=(batch,),
            in_specs=[pl.BlockSpec(memory_space=pltpu.SMEM), row, row, row,
                      pl.BlockSpec(memory_space=pl.ANY), pl.BlockSpec(memory_space=pl.ANY)],
            out_specs=row,
            scratch_shapes=[buf, buf, pltpu.SemaphoreType.DMA((2,))]),
        out_shape=jax.ShapeDtypeStruct((batch, A_HEADS, A_HEAD_DIM), F32),
        compiler_params=_params(("arbitrary",)),
        name="moba_sample",
    )(sel_flat, pt_flat, rel_bias, q3, kn3, vn3, cache_kt, cache_vt)


def _hgrn_sample_kernel(qr_ref, fr_ref, ir_ref, lb_ref, s_ref, o_ref, sn_ref):
    row8 = lax.broadcasted_iota(jnp.int32, (8, LANES), 0)
    eye = (lax.broadcasted_iota(jnp.int32, (LANES, LANES), 0)
           == lax.broadcasted_iota(jnp.int32, (LANES, LANES), 1)).astype(F32)
    outs = []
    for h in range(R_HEADS):
        sl = slice(h * R_KEY_DIM, (h + 1) * R_KEY_DIM)
        q, f = _hgrn_gates(qr_ref[:, sl], fr_ref[:, sl], lb_ref[:, sl])
        k = 1.0 - f
        v = ir_ref[:, sl]
        rows = jnp.where(row8 == 0, q * f, jnp.where(row8 == 1, f, jnp.where(row8 == 2, k, 0.0)))
        cols = _dot_nt(eye, rows, HIGHEST)
        s = s_ref[h]
        outs.append(jnp.sum(s * cols[:, 0:1], axis=0, keepdims=True) + jnp.sum(q * k, axis=-1, keepdims=True) * v)
        sn_ref[h] = s * cols[:, 1:2] + cols[:, 2:3] * v
    o_ref[...] = jnp.concatenate(outs, axis=-1)


def _hgrn_sample(z3, lb_param, state):
    batch = z3.shape[0]
    col = lambda cb: pl.BlockSpec((None, 1, R_WIDTH), lambda b: (b, 0, cb))
    return pl.pallas_call(
        _hgrn_sample_kernel,
        grid=(batch,),
        in_specs=[col(3), col(4), col(5), pl.BlockSpec(lb_param.shape, lambda b: (0, 0)),
                  pl.BlockSpec((None, R_HEADS, R_KEY_DIM, R_VAL_DIM), lambda b: (b, 0, 0, 0))],
        out_specs=[pl.BlockSpec((None, 1, R_WIDTH), lambda b: (b, 0, 0)),
                   pl.BlockSpec((None, R_HEADS, R_KEY_DIM, R_VAL_DIM), lambda b: (b, 0, 0, 0))],
        out_shape=[jax.ShapeDtypeStruct((batch, 1, R_WIDTH), F32),
                   jax.ShapeDtypeStruct(state.shape, F32)],
        compiler_params=_params(("arbitrary",)),
        name="hgrn_sample",
    )(z3, z3, z3, lb_param, state)


def _row_tile(n, preferred):
    t = min(n, preferred)
    while n % t:
        t //= 2
    return t


def kernel(x_prompt, x_sample, c_prompt, c_sample, cache_k, cache_v, state_hgrn, page_table, rel_bias, lb_param, w_ada, b_ada, g_norm1, w_in, g_qnorm, g_knorm, w_a_up, w_r_up, g_onorm, w_out, g_norm2, w_pq, sub_keys, expert_u, expert_v):
    assert w_ada.shape[0] == 1, "single-layer trunk"
    bp, t, d = x_prompt.shape
    bs, ts, _ = x_sample.shape
    n_pages = page_table.shape[1]
    assert ts == 1 and t % MOBA_BLOCK == 0 and n_pages % PAGES_PER_STEP == 0
    assert n_pages // PAGES_PER_BLOCK >= MOBA_TOPK

    w_hi, w_lo = _split_bf16(w_in[0])
    w_lo = w_lo[:, :2 * COL_BLOCK]
    gqk = jnp.stack([jnp.tile(g_qnorm[0], A_HEADS), jnp.tile(g_knorm[0], A_HEADS)]).reshape(2, 1, A_WIDTH)
    head_of = np.arange(A_WIDTH) // A_HEAD_DIM
    bd = jnp.asarray(head_of[:, None] == head_of[None, :], BF16)
    wa, wr, wo, wpq = (w[0].astype(BF16) for w in (w_a_up, w_r_up, w_out, w_pq))
    sk2 = jnp.zeros((P_HEADS, 2 * P_NKEYS, P_KEY_DIM), F32)
    sk2 = sk2.at[:, :P_NKEYS, :P_HALF].set(sub_keys[0, :, 0]).at[:, P_NKEYS:, P_HALF:].set(sub_keys[0, :, 1])
    sk_hi, sk_lo = _split_bf16(sk2)
    u = expert_u[0].astype(BF16)
    vt = expert_v[0].T.astype(BF16)
    g1, g2, go = g_norm1[0].reshape(1, d), g_norm2[0].reshape(1, d), g_onorm[0].reshape(1, R_VAL_DIM)

    n_c = bp + bs
    c_all = jnp.concatenate([c_prompt, c_sample, jnp.zeros((-n_c % 8, d), F32)], axis=0)
    mod = _ada(c_all, w_ada[0], b_ada[0])
    mod_p = mod[:bp].reshape(bp, 6, 1, d)
    mod_s = mod[bp:n_c].reshape(1, bs, 6, d)
    sh1p, sc1p, gt1p, sh2p, sc2p, gt2p = (mod_p[:, i] for i in range(6))
    sh1s, sc1s, gt1s, sh2s, sc2s, gt2s = (mod_s[:, :, i] for i in range(6))
    bias_t = _bias_tiles(rel_bias)

    xp = x_prompt.reshape(bp * t, d)
    zp = _inproj(xp, sc1p, sh1p, g1, w_hi, w_lo, gqk, bd, _row_tile(t, 1024))
    nb = t // MOBA_BLOCK
    vt_p = zp[:, 2 * A_WIDTH:3 * A_WIDTH].reshape(bp, nb, MOBA_BLOCK, A_HEADS // MOBA_HEADS_PER_STEP, MOBA_GROUP_WIDTH)
    vt_p = vt_p.transpose(0, 3, 1, 4, 2)
    oa_p = _moba_prompt(zp.reshape(bp, nb, MOBA_BLOCK, IN_WIDTH), vt_p, bias_t).reshape(bp * t, A_WIDTH)
    or_p, s_prompt = _hgrn_prompt(zp, lb_param, bp)
    x1p = _merge(oa_p, or_p, zp, xp, gt1p, go, wa, wr, wo, _row_tile(t, 512))
    y_prompt = _peer(x1p, sc2p, sh2p, gt2p, g2, wpq, sk_hi, sk_lo, u, vt, _row_tile(t, 1024))

    xs = x_sample.reshape(bs, d)
    zs = _inproj(xs, sc1s, sh1s, g1, w_hi, w_lo, gqk, bd, bs)
    z3 = zs.reshape(bs, 1, IN_WIDTH)
    pt_flat = page_table.reshape(-1)
    ck_t = jnp.transpose(cache_k[0], (0, 2, 3, 1))
    cv_t = jnp.transpose(cache_v[0], (0, 2, 3, 1))
    new3 = lambda lo: zs[:, lo:lo + A_WIDTH].reshape(bs, A_HEADS, A_HEAD_DIM)
    q3 = new3(0)
    sel = _cache_select(q3[..., None], ck_t, pt_flat, n_pages)[:, :, :MOBA_TOPK].reshape(-1)
    oa_s = _moba_sample(q3, new3(A_WIDTH), new3(2 * A_WIDTH), ck_t, cv_t, sel, pt_flat, rel_bias,
                        n_pages).reshape(bs, A_WIDTH)
    or_s, s_sample = _hgrn_sample(z3, lb_param, state_hgrn[0])
    x1s = _merge(oa_s, or_s.reshape(bs, R_WIDTH), zs, xs, gt1s, go, wa, wr, wo, bs)
    y_sample = _peer(x1s, sc2s, sh2s, gt2s, g2, wpq, sk_hi, sk_lo, u, vt, bs)

    heads = lambda z2, lo, lead: z2[:, lo:lo + A_WIDTH].reshape(1, *lead, A_HEADS, A_HEAD_DIM)
    return (y_prompt.reshape(bp, t, d), y_sample.reshape(bs, 1, d),
            heads(zp, A_WIDTH, (bp, t)), heads(zp, 2 * A_WIDTH, (bp, t)), s_prompt[None],
            heads(zs, A_WIDTH, (bs, 1)), heads(zs, 2 * A_WIDTH, (bs, 1)), s_sample[None])
```

```python
import functools
import math

import numpy as np
import jax
import jax.numpy as jnp
from jax import lax
from jax.experimental import pallas as pl
from jax.experimental.pallas import tpu as pltpu

F32 = jnp.float32
BF16 = jnp.bfloat16
HIGHEST = lax.Precision.HIGHEST

D_MODEL = 1024
EPS = 1e-6
PAGE_SIZE = 128
A_HEADS = 8
A_HEAD_DIM = 64
A_WIDTH = A_HEADS * A_HEAD_DIM
A_SCALE = A_HEAD_DIM ** -0.5
MOBA_BLOCK = 256
MOBA_TOPK = 3
NUM_BUCKETS = 32
MAX_DISTANCE = 128
R_HEADS = 4
R_KEY_DIM = 128
R_VAL_DIM = 128
R_WIDTH = R_HEADS * R_KEY_DIM
R_CHUNK = 32
P_HEADS = 8
P_NKEYS = 128
P_KEY_DIM = 128
P_HALF = P_KEY_DIM // 2
P_TOPK = 16
IN_WIDTH = 3 * A_WIDTH + 4 * R_WIDTH + 2 * D_MODEL
COL_BLOCK = 512
N_COL_BLOCKS = IN_WIDTH // COL_BLOCK
NEG = -1e30
VMEM_LIMIT = 56 * 1024 * 1024


def _dot(a, b, precision=None):
    return jnp.dot(a, b, preferred_element_type=F32, precision=precision)


def _dot_nt(a, b, precision=None):
    return lax.dot_general(a, b, (((1,), (1,)), ((), ())), preferred_element_type=F32, precision=precision)


def _silu(x):
    return x * jax.nn.sigmoid(x)


def _params(sem, vmem=VMEM_LIMIT):
    return pltpu.CompilerParams(dimension_semantics=sem, vmem_limit_bytes=vmem)


def _ada_kernel(c_ref, w_ref, b_ref, o_ref):
    o_ref[...] = _dot(_silu(c_ref[...]), w_ref[...], HIGHEST) + b_ref[...]


def _ada(c, w_ada, b_ada):
    n, d = c.shape
    width = w_ada.shape[1]
    tn = 768
    return pl.pallas_call(
        _ada_kernel,
        grid=(width // tn,),
        in_specs=[pl.BlockSpec((n, d), lambda j: (0, 0)),
                  pl.BlockSpec((d, tn), lambda j: (0, j)),
                  pl.BlockSpec((1, tn), lambda j: (0, j))],
        out_specs=pl.BlockSpec((n, tn), lambda j: (0, j)),
        out_shape=jax.ShapeDtypeStruct((n, width), F32),
        compiler_params=_params(("arbitrary",)),
        name="ada_mod",
    )(c, w_ada, b_ada.reshape(1, width))


def _bucket_starts():
    max_exact = NUM_BUCKETS // 2
    n = np.arange(0, 4 * MAX_DISTANCE)
    nf = np.maximum(n, 1).astype(np.float32)
    large = max_exact + (np.log(nf / np.float32(max_exact)) / np.float32(math.log(MAX_DISTANCE / max_exact))
                         * np.float32(NUM_BUCKETS - max_exact)).astype(np.int32)
    bucket = np.where(n < max_exact, n, np.minimum(large, NUM_BUCKETS - 1))
    assert (np.diff(bucket) >= 0).all() and bucket[-1] == NUM_BUCKETS - 1
    return [int(np.argmax(bucket >= b)) for b in range(NUM_BUCKETS)]


_BUCKET_START = _bucket_starts()


def _bias_of_distance(dist, rb_ref, h):
    val = jnp.full(dist.shape, rb_ref[NUM_BUCKETS - 1, h], F32)
    for b in range(NUM_BUCKETS - 2, -1, -1):
        val = jnp.where(dist < _BUCKET_START[b + 1], rb_ref[b, h], val)
    return val


def _bias_kernel(rb_ref, o_ref):
    h = pl.program_id(0)
    d = pl.program_id(1)
    k = lax.broadcasted_iota(jnp.int32, (MOBA_BLOCK, MOBA_BLOCK), 0)
    q = lax.broadcasted_iota(jnp.int32, (MOBA_BLOCK, MOBA_BLOCK), 1)
    dist = d * MOBA_BLOCK + q - k
    val = _bias_of_distance(dist, rb_ref, h)
    o_ref[...] = jnp.where(dist >= 0, val, NEG)


def _bias_tiles(rel_bias):
    return pl.pallas_call(
        _bias_kernel,
        grid=(A_HEADS, 3),
        in_specs=[pl.BlockSpec(memory_space=pltpu.SMEM)],
        out_specs=pl.BlockSpec((None, None, MOBA_BLOCK, MOBA_BLOCK), lambda h, d: (h, d, 0, 0)),
        out_shape=jax.ShapeDtypeStruct((A_HEADS, 3, MOBA_BLOCK, MOBA_BLOCK), F32),
        compiler_params=_params(("arbitrary", "arbitrary")),
        name="moba_bias_tiles",
    )(rel_bias)


def _rms_mod(x, g, sc, sh):
    ms = jnp.mean(x * x, axis=-1, keepdims=True)
    return (x * lax.rsqrt(ms + EPS) * g) * (1.0 + sc) + sh


def _split_bf16(x):
    hi = x.astype(BF16)
    return hi, (x - hi.astype(F32)).astype(BF16)


def _inproj_kernel(x_ref, sc_ref, sh_ref, g_ref, whi_ref, wlo_ref, gqk_ref, bd_ref, o_ref, hhi_ref, hlo_ref):
    j = pl.program_id(1)

    @pl.when(j == 0)
    def _():
        hhi, hlo = _split_bf16(_rms_mod(x_ref[...], g_ref[...], sc_ref[...], sh_ref[...]))
        hhi_ref[...] = hhi
        hlo_ref[...] = hlo

    @pl.when(j < 2)
    def _():
        hhi = hhi_ref[...]
        whi = whi_ref[...]
        z = _dot(hhi, whi) + (_dot(hhi, wlo_ref[...]) + _dot(hlo_ref[...], whi))
        zhi, zlo = _split_bf16(z * z)
        ssq = _dot(zhi, bd_ref[...]) + _dot(zlo, bd_ref[...])
        o_ref[...] = z * lax.rsqrt(ssq * (1.0 / A_HEAD_DIM) + EPS) * gqk_ref[...]

    @pl.when(j >= 2)
    def _():
        o_ref[...] = _dot(hhi_ref[...], whi_ref[...])


def _inproj(x2d, sc, sh, g, whi, wlo, gqk, bd, tm):
    n, d = x2d.shape
    groups, r, _ = sc.shape
    tiles_per_group = n // tm // groups
    mod_spec = pl.BlockSpec((None, r, d), lambda i, j: (i // tiles_per_group, 0, 0))
    return pl.pallas_call(
        _inproj_kernel,
        grid=(n // tm, N_COL_BLOCKS),
        in_specs=[pl.BlockSpec((tm, d), lambda i, j: (i, 0)),
                  mod_spec, mod_spec,
                  pl.BlockSpec((1, d), lambda i, j: (0, 0)),
                  pl.BlockSpec((d, COL_BLOCK), lambda i, j: (0, j)),
                  pl.BlockSpec((d, COL_BLOCK), lambda i, j: (0, jnp.minimum(j, 1))),
                  pl.BlockSpec((None, 1, COL_BLOCK), lambda i, j: (jnp.minimum(j, 1), 0, 0)),
                  pl.BlockSpec((COL_BLOCK, COL_BLOCK), lambda i, j: (0, 0))],
        out_specs=pl.BlockSpec((tm, COL_BLOCK), lambda i, j: (i, j)),
        out_shape=jax.ShapeDtypeStruct((n, IN_WIDTH), F32),
        scratch_shapes=[pltpu.VMEM((tm, d), BF16), pltpu.VMEM((tm, d), BF16)],
        compiler_params=_params(("arbitrary", "arbitrary")),
        name="in_proj",
    )(x2d, sc, sh, g, whi, wlo, gqk, bd)


def _third_largest(s):
    m = jnp.max(s, axis=0, keepdims=True)
    for _ in range(MOBA_TOPK - 1):
        s = jnp.where(s >= m, NEG, s)
        m = jnp.max(s, axis=0, keepdims=True)
    return m


MOBA_HEADS_PER_STEP = 4
MOBA_GROUP_WIDTH = MOBA_HEADS_PER_STEP * A_HEAD_DIM


def _moba_prompt_kernel(q_ref, k_ref, vt_ref, bias_ref, o_ref, km_ref, selb_ref, s_ref, acc_ref):
    qi = pl.program_id(2)
    nb, tk, _ = k_ref.shape
    tq = q_ref.shape[0]
    nh = MOBA_HEADS_PER_STEP

    @pl.when(qi == 0)
    def _():
        km_ref[...] = jnp.mean(k_ref[...], axis=1)

    q = q_ref[...]
    lane = lax.broadcasted_iota(jnp.int32, (1, MOBA_GROUP_WIDTH), 1)
    head_lanes = [(lane // A_HEAD_DIM) == h for h in range(nh)]
    blk = lax.broadcasted_iota(jnp.int32, (nb, tq), 0)
    for h in range(nh):
        s = _dot_nt(jnp.where(head_lanes[h], km_ref[...], 0.0), q, HIGHEST)
        s = jnp.where(blk < qi, s, NEG)
        sel = (blk < qi) & (s >= _third_largest(s))
        selb_ref[h] = jnp.where(sel | (blk == qi), 0.0, NEG)

    qs = (q * A_SCALE).astype(BF16)

    n_blocks = qi + 1

    def by_pairs(step, init):
        c = lax.fori_loop(0, n_blocks // 2, lambda i, c: step(2 * i + 1, step(2 * i, c)), init)
        return lax.cond(n_blocks % 2 == 1, lambda c: step(n_blocks - 1, c), lambda c: c, c)

    def logits_of(j, m):
        kb = k_ref[j]
        dslot = jnp.minimum(qi - j, 2)
        out = []
        for h in range(nh):
            kh = jnp.where(head_lanes[h], kb, 0.0).astype(BF16)
            lg = _dot_nt(kh, qs) + bias_ref[h, dslot] + selb_ref[h, pl.ds(j, 1), :]
            s_ref[j, h] = lg
            out.append(jnp.maximum(m[h], jnp.max(lg, axis=0, keepdims=True)))
        return tuple(out)

    m = by_pairs(logits_of, tuple(jnp.full((1, tq), NEG, F32) for _ in range(nh)))

    acc_ref[...] = jnp.zeros_like(acc_ref)

    def weigh(j, l):
        vtb = vt_ref[j].astype(BF16)
        out = []
        for h in range(nh):
            p = jnp.exp(s_ref[j, h] - m[h])
            acc_ref[h] += _dot(vtb[h * A_HEAD_DIM:(h + 1) * A_HEAD_DIM, :], p.astype(BF16))
            out.append(l[h] + jnp.sum(p, axis=0, keepdims=True))
        return tuple(out)

    l = by_pairs(weigh, tuple(jnp.zeros((1, tq), F32) for _ in range(nh)))
    ot = jnp.concatenate([acc_ref[h] / l[h] for h in range(nh)], axis=0)
    o_ref[...] = ot.T


def _moba_prompt(z4, vt, bias_t):
    b, nb, tk, _ = z4.shape
    groups = A_HEADS // MOBA_HEADS_PER_STEP
    w = MOBA_GROUP_WIDTH
    return pl.pallas_call(
        _moba_prompt_kernel,
        grid=(b, groups, nb),
        in_specs=[pl.BlockSpec((None, None, tk, w), lambda bi, g, qi: (bi, qi, 0, g)),
                  pl.BlockSpec((None, nb, tk, w), lambda bi, g, qi: (bi, 0, 0, groups + g)),
                  pl.BlockSpec((None, None, nb, w, tk), lambda bi, g, qi: (bi, g, 0, 0, 0)),
                  pl.BlockSpec((MOBA_HEADS_PER_STEP, 3, tk, tk), lambda bi, g, qi: (g, 0, 0, 0))],
        out_specs=pl.BlockSpec((None, None, tk, w), lambda bi, g, qi: (bi, qi, 0, g)),
        out_shape=jax.ShapeDtypeStruct((b, nb, tk, A_WIDTH), F32),
        scratch_shapes=[pltpu.VMEM((nb, w), F32), pltpu.VMEM((MOBA_HEADS_PER_STEP, nb, tk), F32),
                        pltpu.VMEM((nb, MOBA_HEADS_PER_STEP, tk, tk), F32),
                        pltpu.VMEM((MOBA_HEADS_PER_STEP, A_HEAD_DIM, tk), F32)],
        compiler_params=_params(("arbitrary", "arbitrary", "arbitrary")),
        name="moba_prompt",
    )(z4, z4, vt, bias_t)


def _hgrn_gates(qr, fr, lb_param):
    e = jnp.exp(lb_param - jnp.max(lb_param, axis=0, keepdims=True))
    lb = e[0:1, :] / jnp.sum(e, axis=0, keepdims=True)
    f = lb + (1.0 - lb) * jax.nn.sigmoid(fr)
    return _silu(qr), f


def _hgrn_prompt_kernel(qr_ref, fr_ref, ir_ref, lb_ref, o_ref, s_ref, st_ref):
    tb = pl.program_id(1)
    t = qr_ref.shape[0]
    c = R_CHUNK
    row = lax.broadcasted_iota(jnp.int32, (c, c), 0)
    col = lax.broadcasted_iota(jnp.int32, (c, c), 1)
    tril = row >= col
    tril_f = tril.astype(F32)
    lb = lb_ref[...]

    @pl.when(tb == 0)
    def _():
        st_ref[...] = jnp.zeros_like(st_ref)

    def chunk(ci, _):
        rows = pl.ds(pl.multiple_of(ci * c, c), c)
        q, f = _hgrn_gates(qr_ref[rows, :], fr_ref[rows, :], lb)
        k = 1.0 - f
        v = ir_ref[rows, :]
        b = _dot(tril_f, jnp.log(f), HIGHEST)
        bl = b[c - 1:c, :]
        qe = (q * jnp.exp(b)).astype(BF16)
        ke = (k * jnp.exp(-b)).astype(BF16)
        kd = (k * jnp.exp(bl - b)).astype(BF16)
        decay = jnp.exp(bl)
        outs = []
        for h in range(R_HEADS):
            sl = slice(h * R_KEY_DIM, (h + 1) * R_KEY_DIM)
            st = st_ref[h]
            vh = v[:, sl]
            att = jnp.where(tril, _dot_nt(qe[:, sl], ke[:, sl]), 0.0)
            outs.append(_dot_nt(qe[:, sl], st.astype(BF16)) + _dot(att.astype(BF16), vh.astype(BF16)))
            st_ref[h] = st * decay[:, sl] + _dot(vh.T.astype(BF16), kd[:, sl])
        o_ref[rows, :] = jnp.concatenate(outs, axis=-1)
        return 0

    lax.fori_loop(0, t // c, chunk, 0, unroll=2)

    @pl.when(tb == pl.num_programs(1) - 1)
    def _():
        for h in range(R_HEADS):
            s_ref[h] = st_ref[h].T


def _hgrn_prompt(z2d, lb_param, batch):
    n = z2d.shape[0]
    t = n // batch
    tb = _row_tile(t, 1024)
    col0 = 3 * A_WIDTH // R_WIDTH
    spec = lambda off: pl.BlockSpec((tb, R_WIDTH), lambda bi, ti: (bi * (t // tb) + ti, col0 + off))
    state = pl.BlockSpec((None, R_HEADS, R_KEY_DIM, R_VAL_DIM), lambda bi, ti: (bi, 0, 0, 0))
    return pl.pallas_call(
        _hgrn_prompt_kernel,
        grid=(batch, t // tb),
        in_specs=[spec(0), spec(1), spec(2), pl.BlockSpec(lb_param.shape, lambda bi, ti: (0, 0))],
        out_specs=[pl.BlockSpec((tb, R_WIDTH), lambda bi, ti: (bi * (t // tb) + ti, 0)), state],
        out_shape=[jax.ShapeDtypeStruct((n, R_WIDTH), F32),
                   jax.ShapeDtypeStruct((batch, R_HEADS, R_KEY_DIM, R_VAL_DIM), F32)],
        scratch_shapes=[pltpu.VMEM((R_HEADS, R_VAL_DIM, R_KEY_DIM), F32)],
        compiler_params=_params(("arbitrary", "arbitrary")),
        name="hgrn_prompt",
    )(z2d, z2d, z2d, lb_param)


def _merge_kernel(oa_ref, or_ref, gr_ref, ga0_ref, ga1_ref, gb0_ref, gb1_ref, x_ref, gt_ref, go_ref,
                  wa_ref, wr_ref, wo_ref, o_ref):
    o_r = or_ref[...]
    go = go_ref[...]
    heads = []
    for h in range(R_HEADS):
        v = o_r[:, h * R_VAL_DIM:(h + 1) * R_VAL_DIM]
        heads.append(v * lax.rsqrt(jnp.mean(v * v, axis=-1, keepdims=True) + EPS) * go)
    o_r = jnp.concatenate(heads, axis=-1) * _silu(gr_ref[...])
    ga = jnp.concatenate([ga0_ref[...], ga1_ref[...]], axis=-1)
    gb = jnp.concatenate([gb0_ref[...], gb1_ref[...]], axis=-1)
    merged = (jax.nn.sigmoid(ga) * _dot(oa_ref[...].astype(BF16), wa_ref[...])
              + jax.nn.sigmoid(gb) * _dot(o_r.astype(BF16), wr_ref[...]))
    o_ref[...] = x_ref[...] + gt_ref[...] * _dot(merged.astype(BF16), wo_ref[...])


def _merge(o_a, o_r, z2d, x2d, gt, g_o, wa, wr, wo, tm):
    n, d = x2d.shape
    groups, r, _ = gt.shape
    tiles_per_group = n // tm // groups
    zcol = lambda cb: pl.BlockSpec((tm, COL_BLOCK), lambda i: (i, cb))
    full = lambda a: pl.BlockSpec(a.shape, lambda i: (0,) * a.ndim)
    return pl.pallas_call(
        _merge_kernel,
        grid=(n // tm,),
        in_specs=[pl.BlockSpec((tm, A_WIDTH), lambda i: (i, 0)),
                  pl.BlockSpec((tm, R_WIDTH), lambda i: (i, 0)),
                  zcol(6), zcol(7), zcol(8), zcol(9), zcol(10),
                  pl.BlockSpec((tm, d), lambda i: (i, 0)),
                  pl.BlockSpec((None, r, d), lambda i: (i // tiles_per_group, 0, 0)),
                  full(g_o), full(wa), full(wr), full(wo)],
        out_specs=pl.BlockSpec((tm, d), lambda i: (i, 0)),
        out_shape=jax.ShapeDtypeStruct((n, d), F32),
        compiler_params=_params(("arbitrary",)),
        name="merge_out_proj",
    )(o_a, o_r, z2d, z2d, z2d, z2d, z2d, x2d, gt, g_o, wa, wr, wo)


_STAIR = [P_TOPK // (a + 1) for a in range(P_TOPK)]
LANES = 128
ROWS_PER_BLOCK = 4
E_BLOCK = ROWS_PER_BLOCK * P_NKEYS


def _top_rows(s, n):
    rows = []
    for r in range(n):
        m = jnp.max(s, axis=0, keepdims=True)
        rows.append(m)
        if r + 1 < n:
            s = jnp.where(s >= m, NEG, s)
    return rows


def _stack_rows(rows):
    n = len(rows)
    idx = lax.broadcasted_iota(jnp.int32, (n, LANES), 0)
    out = jnp.zeros((n, LANES), F32)
    for r, row in enumerate(rows):
        out = jnp.where(idx == r, row, out)
    return out


def _top_rows_ranked(s, n):
    rows = []
    rank = jnp.full(s.shape, float(n), F32)
    for r in range(n):
        m = jnp.max(s, axis=0, keepdims=True)
        rows.append(m)
        hit = s >= m
        rank = jnp.where(hit, float(r), rank)
        s = jnp.where(hit, NEG, s)
    return rows, rank


def _route(s0, s1):
    row8 = lax.broadcasted_iota(jnp.int32, (8, LANES), 0)
    v0 = _top_rows(s0, P_TOPK)
    v1, rank1 = _top_rows_ranked(s1, P_TOPK)
    v1_all = _stack_rows(v1)
    cands = [v0[0] + v1_all, v0[1] + v1_all[:8]]
    cands += [jnp.where(row8 < _STAIR[a], v0[a] + v1_all[:8], NEG) for a in range(2, P_TOPK)]
    cand = jnp.concatenate(cands, axis=0)
    tau = _top_rows(cand, P_TOPK)[-1]
    top = v0[0] + v1[0]
    z = jnp.sum(jnp.where(cand >= tau, jnp.exp(cand - top), 0.0), axis=0, keepdims=True)
    n = jnp.zeros(s0.shape, F32)
    for a in range(P_TOPK):
        n_a = jnp.sum(jnp.where(cands[a] >= tau, 1.0, 0.0), axis=0, keepdims=True)
        n = jnp.where(s0 == v0[a], n_a, n)
    return n, jnp.exp(s0 - v0[0]) / z, rank1.astype(BF16), jnp.exp(s1 - v1[0]).astype(BF16)


def _peer_kernel(x_ref, sc_ref, sh_ref, gt_ref, g_ref, wpq_ref, skh_ref, skl_ref, u0_ref, ua_ref, ub_ref,
                 vta_ref, vtb_ref, o_ref, h2_ref, n_ref, e0_ref, r1_ref, e1_ref, a0_ref, a1_ref, g0_ref, g1_ref,
                 acc_ref):
    s = pl.program_id(1)
    tm = x_ref.shape[0]
    nc = tm // LANES
    group = min(nc, 2)

    @pl.when(s == 0)
    def _():
        h2 = _rms_mod(x_ref[...], g_ref[...], sc_ref[...], sh_ref[...]).astype(BF16)
        h2_ref[...] = h2
        q = _dot(h2, wpq_ref[...])
        for h in range(P_HEADS):
            qh, ql = _split_bf16(q[:, h * P_KEY_DIM:(h + 1) * P_KEY_DIM])
            skh = skh_ref[h]
            st = _dot_nt(skh, qh) + (_dot_nt(skh, ql) + _dot_nt(skl_ref[h], qh))
            for c in range(nc):
                n_ref[h * nc + c] = st[:P_NKEYS, c * LANES:(c + 1) * LANES]
                e0_ref[h * nc + c] = st[P_NKEYS:, c * LANES:(c + 1) * LANES]

        def route(pair, _):
            for idx in (2 * pair, 2 * pair + 1):
                n, e0, r1, e1 = _route(n_ref[idx], e0_ref[idx])
                n_ref[idx] = n
                e0_ref[idx] = 0.5 * e0
                r1_ref[idx] = r1.reshape(P_NKEYS // 16, 16, LANES)
                e1_ref[idx] = e1.reshape(P_NKEYS // 16, 16, LANES)
            return 0

        lax.fori_loop(0, P_HEADS * nc // 2, route, 0)
        acc_ref[...] = jnp.zeros_like(acc_ref)
        a0_ref[...] = _dot_nt(u0_ref[...], h2)

    def gate(blk, a_ref, g_ref_, c):
        w = [jnp.zeros((P_NKEYS // 16, 16, LANES), BF16) for _ in range(ROWS_PER_BLOCK)]
        for h in range(P_HEADS):
            idx = h * nc + c
            r1 = r1_ref[idx]
            e1 = e1_ref[idx]
            for ii in range(ROWS_PER_BLOCK):
                i = ROWS_PER_BLOCK * blk + ii
                n_row = jnp.broadcast_to(n_ref[idx, pl.ds(i, 1), :], (16, LANES)).astype(BF16)
                e0_row = jnp.broadcast_to(e0_ref[idx, pl.ds(i, 1), :], (16, LANES)).astype(BF16)
                w[ii] = w[ii] + jnp.where(r1 < n_row[None], e1, jnp.zeros((), BF16)) * e0_row[None]
        for ii in range(ROWS_PER_BLOCK):
            a = a_ref[ii * P_NKEYS:(ii + 1) * P_NKEYS, c * LANES:(c + 1) * LANES]
            act = (a * (1.0 + lax.erf(a * (1.0 / math.sqrt(2.0))))).astype(BF16)
            g_ref_[ii * P_NKEYS:(ii + 1) * P_NKEYS, c * LANES:(c + 1) * LANES] = act * w[ii].reshape(P_NKEYS, LANES)

    def half_step(blk, a_src, a_dst, u_next, g_dst, vt_cur):
        for cg in range(nc // group):
            cols = slice(cg * group * LANES, (cg + 1) * group * LANES)
            a_dst[:, cols] = _dot_nt(u_next[...], h2_ref[cols, :])
            for c in range(cg * group, (cg + 1) * group):
                gate(blk, a_src, g_dst, c)
            acc_ref[:, cols] += _dot(vt_cur[...], g_dst[:, cols])

    half_step(2 * s, a0_ref, a1_ref, ua_ref, g0_ref, vta_ref)
    half_step(2 * s + 1, a1_ref, a0_ref, ub_ref, g1_ref, vtb_ref)

    @pl.when(s == pl.num_programs(1) - 1)
    def _():
        o_ref[...] = x_ref[...] + gt_ref[...] * acc_ref[...].T


def _peer(x2d, sc, sh, gt, g, wpq, skh, skl, u, vt, tm):
    n, d = x2d.shape
    groups, r, _ = sc.shape
    tiles_per_group = n // tm // groups
    nc = tm // LANES
    n_blk = u.shape[0] // E_BLOCK
    mod_spec = pl.BlockSpec((None, r, d), lambda i, s: (i // tiles_per_group, 0, 0))
    route32 = pltpu.VMEM((P_HEADS * nc, P_NKEYS, LANES), F32)
    route16 = pltpu.VMEM((P_HEADS * nc, P_NKEYS // 16, 16, LANES), BF16)
    pre = pltpu.VMEM((E_BLOCK, tm), F32)
    gated = pltpu.VMEM((E_BLOCK, tm), BF16)
    single = pl.Buffered(1)
    return pl.pallas_call(
        _peer_kernel,
        grid=(n // tm, n_blk // 2),
        in_specs=[pl.BlockSpec((tm, d), lambda i, s: (i, 0), pipeline_mode=single),
                  mod_spec, mod_spec, mod_spec,
                  pl.BlockSpec((1, d), lambda i, s: (0, 0)),
                  pl.BlockSpec(wpq.shape, lambda i, s: (0, 0), pipeline_mode=single),
                  pl.BlockSpec(skh.shape, lambda i, s: (0, 0, 0), pipeline_mode=single),
                  pl.BlockSpec(skl.shape, lambda i, s: (0, 0, 0), pipeline_mode=single),
                  pl.BlockSpec((E_BLOCK, d), lambda i, s: (0, 0), pipeline_mode=single),
                  pl.BlockSpec((E_BLOCK, d), lambda i, s: (2 * s + 1, 0)),
                  pl.BlockSpec((E_BLOCK, d), lambda i, s: (jnp.minimum(2 * s + 2, n_blk - 1), 0)),
                  pl.BlockSpec((None, d, E_BLOCK), lambda i, s: (2 * s, 0, 0)),
                  pl.BlockSpec((None, d, E_BLOCK), lambda i, s: (2 * s + 1, 0, 0))],
        out_specs=pl.BlockSpec((tm, d), lambda i, s: (i, 0), pipeline_mode=single),
        out_shape=jax.ShapeDtypeStruct((n, d), F32),
        scratch_shapes=[pltpu.VMEM((tm, d), BF16), route32, route32, route16, route16, pre, pre, gated, gated,
                        pltpu.VMEM((d, tm), F32)],
        compiler_params=_params(("arbitrary", "arbitrary")),
        name="peer",
    )(x2d, sc, sh, gt, g, wpq, skh, skl, u, u, u, vt, vt)


PAGES_PER_STEP = 16
PAGES_PER_BLOCK = MOBA_BLOCK // PAGE_SIZE
TILES = MOBA_TOPK * PAGES_PER_BLOCK
KEYS = TILES * PAGE_SIZE


def _cache_select_kernel(per_step, pt_ref, q_ref, *refs):
    pages, o_ref, s_ref = refs[:per_step], refs[per_step], refs[per_step + 1]
    g = pl.program_id(1)
    blocks_per_step = per_step // PAGES_PER_BLOCK
    row = lax.broadcasted_iota(jnp.int32, (A_HEADS, LANES), 0)
    lane = lax.broadcasted_iota(jnp.int32, (A_HEADS, LANES), 1)

    @pl.when(g == 0)
    def _():
        s_ref[...] = jnp.full(s_ref.shape, NEG, F32)

    s = s_ref[...]
    for blk in range(blocks_per_step):
        per_token = jnp.zeros((A_HEADS, PAGE_SIZE), F32)
        for h in range(A_HEADS):
            prod = pages[blk * PAGES_PER_BLOCK][h] * q_ref[h]
            for p in range(1, PAGES_PER_BLOCK):
                prod = prod + pages[blk * PAGES_PER_BLOCK + p][h] * q_ref[h]
            qk = jnp.sum(prod, axis=0, keepdims=True)
            per_token = jnp.where(row == h, qk, per_token)
        score = jnp.sum(per_token, axis=-1, keepdims=True) * (1.0 / MOBA_BLOCK)
        s = jnp.where(lane == g * blocks_per_step + blk, score, s)
    s_ref[...] = s

    @pl.when(g == pl.num_programs(1) - 1)
    def _():
        sc = s
        out = jnp.zeros((A_HEADS, LANES), jnp.int32)
        for r in range(MOBA_TOPK):
            m = jnp.max(sc, axis=-1, keepdims=True)
            idx = jnp.min(jnp.where(sc >= m, lane, LANES), axis=-1, keepdims=True)
            out = jnp.where(lane == r, idx, out)
            sc = jnp.where(lane == idx, NEG, sc)
        o_ref[...] = out


def _cache_select(q_cols, cache_t, pt_flat, n_pages):
    batch = q_cols.shape[0]
    _, nh, dh, _ = cache_t.shape
    per_step = math.gcd(n_pages, PAGES_PER_STEP)
    steps = n_pages // per_step
    page = lambda p: pl.BlockSpec((None, nh, dh, PAGE_SIZE),
                                  lambda b, g, pt: (pt[b * n_pages + g * per_step + p], 0, 0, 0))
    return pl.pallas_call(
        functools.partial(_cache_select_kernel, per_step),
        grid_spec=pltpu.PrefetchScalarGridSpec(
            num_scalar_prefetch=1,
            grid=(batch, steps),
            in_specs=[pl.BlockSpec((None, nh, dh, 1), lambda b, g, pt: (b, 0, 0, 0))]
                     + [page(p) for p in range(per_step)],
            out_specs=pl.BlockSpec((None, nh, LANES), lambda b, g, pt: (b, 0, 0)),
            scratch_shapes=[pltpu.VMEM((nh, LANES), F32)]),
        out_shape=jax.ShapeDtypeStruct((batch, nh, LANES), jnp.int32),
        compiler_params=_params(("arbitrary", "arbitrary")),
        name="cache_select_blocks",
    )(pt_flat, q_cols, *([cache_t] * per_step))


def _moba_sample_kernel(past_len, n_pages, sel_ref, pt_ref, rb_ref, q_ref, kn_ref, vn_ref, ck_ref, cv_ref, o_ref,
                        kbuf, vbuf, sem):
    b = pl.program_id(0)
    nb = pl.num_programs(0)

    def copies(bb, slot):
        out = []
        for h in range(A_HEADS):
            for r in range(MOBA_TOPK):
                blk = sel_ref[(bb * A_HEADS + h) * MOBA_TOPK + r]
                for p in range(PAGES_PER_BLOCK):
                    page = pt_ref[bb * n_pages + blk * PAGES_PER_BLOCK + p]
                    cols = pl.ds((r * PAGES_PER_BLOCK + p) * PAGE_SIZE, PAGE_SIZE)
                    out.append(pltpu.make_async_copy(ck_ref.at[page, h], kbuf.at[slot, h, :, cols], sem.at[slot]))
                    out.append(pltpu.make_async_copy(cv_ref.at[page, h], vbuf.at[slot, h, :, cols], sem.at[slot]))
        return out

    @pl.when(b == 0)
    def _():
        for c in copies(0, 0):
            c.start()

    @pl.when(b + 1 < nb)
    def _():
        for c in copies(b + 1, (b + 1) % 2):
            c.start()

    slot = b % 2
    for c in copies(b, slot):
        c.wait()

    row8 = lax.broadcasted_iota(jnp.int32, (8, A_HEAD_DIM), 0)
    row8k = lax.broadcasted_iota(jnp.int32, (8, KEYS), 0)
    lane = lax.broadcasted_iota(jnp.int32, (1, KEYS), 1)
    for h in range(A_HEADS):
        qh = q_ref[h:h + 1, :] * A_SCALE
        kn = kn_ref[h:h + 1, :]
        vn = vn_ref[h:h + 1, :]
        q8 = jnp.where(row8 == 0, qh, 0.0).astype(BF16)
        start = [past_len - sel_ref[(b * A_HEADS + h) * MOBA_TOPK + r] * MOBA_BLOCK for r in range(MOBA_TOPK)]
        which = lane // MOBA_BLOCK
        dist = jnp.where(which == 0, start[0], jnp.where(which == 1, start[1], start[2])) - lane % MOBA_BLOCK
        logits = _dot(q8, kbuf[slot, h].astype(BF16))[0:1, :] + _bias_of_distance(dist, rb_ref, h)
        own = jnp.sum(qh * kn, axis=-1, keepdims=True) + rb_ref[0, h]
        m = jnp.maximum(own, jnp.max(logits, axis=-1, keepdims=True))
        p_own = jnp.exp(own - m)
        p = jnp.exp(logits - m)
        l = p_own + jnp.sum(p, axis=-1, keepdims=True)
        p8 = jnp.where(row8k == 0, p, 0.0).astype(BF16)
        o_ref[h:h + 1, :] = (_dot_nt(p8, vbuf[slot, h].astype(BF16))[0:1, :] + p_own * vn) / l


def _moba_sample(q3, kn3, vn3, cache_kt, cache_vt, sel_flat, pt_flat, rel_bias, n_pages):
    batch = q3.shape[0]
    row = pl.BlockSpec((None, A_HEADS, A_HEAD_DIM), lambda b, sel, pt: (b, 0, 0))
    buf = pltpu.VMEM((2, A_HEADS, A_HEAD_DIM, KEYS), F32)
    return pl.pallas_call(
        functools.partial(_moba_sample_kernel, n_pages * PAGE_SIZE, n_pages),
        grid_spec=pltpu.PrefetchScalarGridSpec(
            num_scalar_prefetch=2,
            grid=(batch,),
            in_specs=[pl.BlockSpec(memory_space=pltpu.SMEM), row, row, row,
                      pl.BlockSpec(memory_space=pl.ANY), pl.BlockSpec(memory_space=pl.ANY)],
            out_specs=row,
            scratch_shapes=[buf, buf, pltpu.SemaphoreType.DMA((2,))]),
        out_shape=jax.ShapeDtypeStruct((batch, A_HEADS, A_HEAD_DIM), F32),
        compiler_params=_params(("arbitrary",)),
        name="moba_sample",
    )(sel_flat, pt_flat, rel_bias, q3, kn3, vn3, cache_kt, cache_vt)


def _hgrn_sample_kernel(qr_ref, fr_ref, ir_ref, lb_ref, s_ref, o_ref, sn_ref):
    row8 = lax.broadcasted_iota(jnp.int32, (8, LANES), 0)
    eye = (lax.broadcasted_iota(jnp.int32, (LANES, LANES), 0)
           == lax.broadcasted_iota(jnp.int32, (LANES, LANES), 1)).astype(F32)
    outs = []
    for h in range(R_HEADS):
        sl = slice(h * R_KEY_DIM, (h + 1) * R_KEY_DIM)
        q, f = _hgrn_gates(qr_ref[:, sl], fr_ref[:, sl], lb_ref[:, sl])
        k = 1.0 - f
        v = ir_ref[:, sl]
        rows = jnp.where(row8 == 0, q * f, jnp.where(row8 == 1, f, jnp.where(row8 == 2, k, 0.0)))
        cols = _dot_nt(eye, rows, HIGHEST)
        s = s_ref[h]
        outs.append(jnp.sum(s * cols[:, 0:1], axis=0, keepdims=True) + jnp.sum(q * k, axis=-1, keepdims=True) * v)
        sn_ref[h] = s * cols[:, 1:2] + cols[:, 2:3] * v
    o_ref[...] = jnp.concatenate(outs, axis=-1)


def _hgrn_sample(z3, lb_param, state):
    batch = z3.shape[0]
    col = lambda cb: pl.BlockSpec((None, 1, R_WIDTH), lambda b: (b, 0, cb))
    return pl.pallas_call(
        _hgrn_sample_kernel,
        grid=(batch,),
        in_specs=[col(3), col(4), col(5), pl.BlockSpec(lb_param.shape, lambda b: (0, 0)),
                  pl.BlockSpec((None, R_HEADS, R_KEY_DIM, R_VAL_DIM), lambda b: (b, 0, 0, 0))],
        out_specs=[pl.BlockSpec((None, 1, R_WIDTH), lambda b: (b, 0, 0)),
                   pl.BlockSpec((None, R_HEADS, R_KEY_DIM, R_VAL_DIM), lambda b: (b, 0, 0, 0))],
        out_shape=[jax.ShapeDtypeStruct((batch, 1, R_WIDTH), F32),
                   jax.ShapeDtypeStruct(state.shape, F32)],
        compiler_params=_params(("arbitrary",)),
        name="hgrn_sample",
    )(z3, z3, z3, lb_param, state)


def _row_tile(n, preferred):
    t = min(n, preferred)
    while n % t:
        t //= 2
    return t


def kernel(x_prompt, x_sample, c_prompt, c_sample, cache_k, cache_v, state_hgrn, page_table, rel_bias, lb_param, w_ada, b_ada, g_norm1, w_in, g_qnorm, g_knorm, w_a_up, w_r_up, g_onorm, w_out, g_norm2, w_pq, sub_keys, expert_u, expert_v):
    assert w_ada.shape[0] == 1, "single-layer trunk"
    bp, t, d = x_prompt.shape
    bs, ts, _ = x_sample.shape
    n_pages = page_table.shape[1]
    assert ts == 1 and t % MOBA_BLOCK == 0 and n_pages % PAGES_PER_BLOCK == 0
    assert n_pages // PAGES_PER_BLOCK >= MOBA_TOPK

    w_hi, w_lo = _split_bf16(w_in[0])
    w_lo = w_lo[:, :2 * COL_BLOCK]
    gqk = jnp.stack([jnp.tile(g_qnorm[0], A_HEADS), jnp.tile(g_knorm[0], A_HEADS)]).reshape(2, 1, A_WIDTH)
    head_of = np.arange(A_WIDTH) // A_HEAD_DIM
    bd = jnp.asarray(head_of[:, None] == head_of[None, :], BF16)
    wa, wr, wo, wpq = (w[0].astype(BF16) for w in (w_a_up, w_r_up, w_out, w_pq))
    sk2 = jnp.zeros((P_HEADS, 2 * P_NKEYS, P_KEY_DIM), F32)
    sk2 = sk2.at[:, :P_NKEYS, :P_HALF].set(sub_keys[0, :, 0]).at[:, P_NKEYS:, P_HALF:].set(sub_keys[0, :, 1])
    sk_hi, sk_lo = _split_bf16(sk2)
    u = expert_u[0].astype(BF16)
    vt = expert_v[0].astype(BF16).reshape(-1, E_BLOCK, d).transpose(0, 2, 1)
    g1, g2, go = g_norm1[0].reshape(1, d), g_norm2[0].reshape(1, d), g_onorm[0].reshape(1, R_VAL_DIM)

    n_c = bp + bs
    c_all = jnp.concatenate([c_prompt, c_sample, jnp.zeros((-n_c % 8, d), F32)], axis=0)
    mod = _ada(c_all, w_ada[0], b_ada[0])
    mod_p = mod[:bp].reshape(bp, 6, 1, d)
    mod_s = mod[bp:n_c].reshape(1, bs, 6, d)
    sh1p, sc1p, gt1p, sh2p, sc2p, gt2p = (mod_p[:, i] for i in range(6))
    sh1s, sc1s, gt1s, sh2s, sc2s, gt2s = (mod_s[:, :, i] for i in range(6))
    bias_t = _bias_tiles(rel_bias)

    xp = x_prompt.reshape(bp * t, d)
    zp = _inproj(xp, sc1p, sh1p, g1, w_hi, w_lo, gqk, bd, _row_tile(t, 1024))
    nb = t // MOBA_BLOCK
    vt_p = zp[:, 2 * A_WIDTH:3 * A_WIDTH].reshape(bp, nb, MOBA_BLOCK, A_HEADS // MOBA_HEADS_PER_STEP, MOBA_GROUP_WIDTH)
    vt_p = vt_p.transpose(0, 3, 1, 4, 2)
    oa_p = _moba_prompt(zp.reshape(bp, nb, MOBA_BLOCK, IN_WIDTH), vt_p, bias_t).reshape(bp * t, A_WIDTH)
    or_p, s_prompt = _hgrn_prompt(zp, lb_param, bp)
    x1p = _merge(oa_p, or_p, zp, xp, gt1p, go, wa, wr, wo, _row_tile(t, 512))
    y_prompt = _peer(x1p, sc2p, sh2p, gt2p, g2, wpq, sk_hi, sk_lo, u, vt, _row_tile(t, 1024))

    xs = x_sample.reshape(bs, d)
    zs = _inproj(xs, sc1s, sh1s, g1, w_hi, w_lo, gqk, bd, bs)
    z3 = zs.reshape(bs, 1, IN_WIDTH)
    pt_flat = page_table.reshape(-1)
    ck_t = jnp.transpose(cache_k[0], (0, 2, 3, 1))
    cv_t = jnp.transpose(cache_v[0], (0, 2, 3, 1))
    new3 = lambda lo: zs[:, lo:lo + A_WIDTH].reshape(bs, A_HEADS, A_HEAD_DIM)
    q3 = new3(0)
    sel = _cache_select(q3[..., None], ck_t, pt_flat, n_pages)[:, :, :MOBA_TOPK].reshape(-1)
    oa_s = _moba_sample(q3, new3(A_WIDTH), new3(2 * A_WIDTH), ck_t, cv_t, sel, pt_flat, rel_bias,
                        n_pages).reshape(bs, A_WIDTH)
    or_s, s_sample = _hgrn_sample(z3, lb_param, state_hgrn[0])
    x1s = _merge(oa_s, or_s.reshape(bs, R_WIDTH), zs, xs, gt1s, go, wa, wr, wo, bs)
    y_sample = _peer(x1s, sc2s, sh2s, gt2s, g2, wpq, sk_hi, sk_lo, u, vt, bs)

    heads = lambda z2, lo, lead: z2[:, lo:lo + A_WIDTH].reshape(1, *lead, A_HEADS, A_HEAD_DIM)
    return (y_prompt.reshape(bp, t, d), y_sample.reshape(bs, 1, d),
            heads(zp, A_WIDTH, (bp, t)), heads(zp, 2 * A_WIDTH, (bp, t)), s_prompt[None],
            heads(zs, A_WIDTH, (bs, 1)), heads(zs, 2 * A_WIDTH, (bs, 1)), s_sample[None])
```

```python
import functools
import math

import numpy as np
import jax
import jax.numpy as jnp
from jax import lax
from jax.experimental import pallas as pl
from jax.experimental.pallas import tpu as pltpu

F32 = jnp.float32
BF16 = jnp.bfloat16
HIGHEST = lax.Precision.HIGHEST

D_MODEL = 1024
EPS = 1e-6
PAGE_SIZE = 128
A_HEADS = 8
A_HEAD_DIM = 64
A_WIDTH = A_HEADS * A_HEAD_DIM
A_SCALE = A_HEAD_DIM ** -0.5
MOBA_BLOCK = 256
MOBA_TOPK = 3
NUM_BUCKETS = 32
MAX_DISTANCE = 128
R_HEADS = 4
R_KEY_DIM = 128
R_VAL_DIM = 128
R_WIDTH = R_HEADS * R_KEY_DIM
R_CHUNK = 32
P_HEADS = 8
P_NKEYS = 128
P_KEY_DIM = 128
P_HALF = P_KEY_DIM // 2
P_TOPK = 16
IN_WIDTH = 3 * A_WIDTH + 4 * R_WIDTH + 2 * D_MODEL
COL_BLOCK = 512
N_COL_BLOCKS = IN_WIDTH // COL_BLOCK
NEG = -1e30
VMEM_LIMIT = 56 * 1024 * 1024


def _dot(a, b, precision=None):
    return jnp.dot(a, b, preferred_element_type=F32, precision=precision)


def _dot_nt(a, b, precision=None):
    return lax.dot_general(a, b, (((1,), (1,)), ((), ())), preferred_element_type=F32, precision=precision)


def _silu(x):
    return x * jax.nn.sigmoid(x)


def _params(sem, vmem=VMEM_LIMIT):
    return pltpu.CompilerParams(dimension_semantics=sem, vmem_limit_bytes=vmem)


def _ada_kernel(c_ref, w_ref, b_ref, o_ref):
    o_ref[...] = _dot(_silu(c_ref[...]), w_ref[...], HIGHEST) + b_ref[...]


def _ada(c, w_ada, b_ada):
    n, d = c.shape
    width = w_ada.shape[1]
    tn = 768
    return pl.pallas_call(
        _ada_kernel,
        grid=(width // tn,),
        in_specs=[pl.BlockSpec((n, d), lambda j: (0, 0)),
                  pl.BlockSpec((d, tn), lambda j: (0, j)),
                  pl.BlockSpec((1, tn), lambda j: (0, j))],
        out_specs=pl.BlockSpec((n, tn), lambda j: (0, j)),
        out_shape=jax.ShapeDtypeStruct((n, width), F32),
        compiler_params=_params(("arbitrary",)),
        name="ada_mod",
    )(c, w_ada, b_ada.reshape(1, width))


def _bucket_starts():
    max_exact = NUM_BUCKETS // 2
    n = np.arange(0, 4 * MAX_DISTANCE)
    nf = np.maximum(n, 1).astype(np.float32)
    large = max_exact + (np.log(nf / np.float32(max_exact)) / np.float32(math.log(MAX_DISTANCE / max_exact))
                         * np.float32(NUM_BUCKETS - max_exact)).astype(np.int32)
    bucket = np.where(n < max_exact, n, np.minimum(large, NUM_BUCKETS - 1))
    assert (np.diff(bucket) >= 0).all() and bucket[-1] == NUM_BUCKETS - 1
    return [int(np.argmax(bucket >= b)) for b in range(NUM_BUCKETS)]


_BUCKET_START = _bucket_starts()


def _bias_of_distance(dist, rb_ref, h):
    val = jnp.full(dist.shape, rb_ref[NUM_BUCKETS - 1, h], F32)
    for b in range(NUM_BUCKETS - 2, -1, -1):
        val = jnp.where(dist < _BUCKET_START[b + 1], rb_ref[b, h], val)
    return val


def _bias_kernel(rb_ref, o_ref):
    h = pl.program_id(0)
    d = pl.program_id(1)
    k = lax.broadcasted_iota(jnp.int32, (MOBA_BLOCK, MOBA_BLOCK), 0)
    q = lax.broadcasted_iota(jnp.int32, (MOBA_BLOCK, MOBA_BLOCK), 1)
    dist = d * MOBA_BLOCK + q - k
    val = _bias_of_distance(dist, rb_ref, h)
    o_ref[...] = jnp.where(dist >= 0, val, NEG)


def _bias_tiles(rel_bias):
    return pl.pallas_call(
        _bias_kernel,
        grid=(A_HEADS, 3),
        in_specs=[pl.BlockSpec(memory_space=pltpu.SMEM)],
        out_specs=pl.BlockSpec((None, None, MOBA_BLOCK, MOBA_BLOCK), lambda h, d: (h, d, 0, 0)),
        out_shape=jax.ShapeDtypeStruct((A_HEADS, 3, MOBA_BLOCK, MOBA_BLOCK), F32),
        compiler_params=_params(("arbitrary", "arbitrary")),
        name="moba_bias_tiles",
    )(rel_bias)


def _rms_mod(x, g, sc, sh):
    ms = jnp.mean(x * x, axis=-1, keepdims=True)
    return (x * lax.rsqrt(ms + EPS) * g) * (1.0 + sc) + sh


def _split_bf16(x):
    hi = x.astype(BF16)
    return hi, (x - hi.astype(F32)).astype(BF16)


def _inproj_kernel(x_ref, sc_ref, sh_ref, g_ref, whi_ref, wlo_ref, gqk_ref, bd_ref, o_ref, hhi_ref, hlo_ref):
    j = pl.program_id(1)

    @pl.when(j == 0)
    def _():
        hhi, hlo = _split_bf16(_rms_mod(x_ref[...], g_ref[...], sc_ref[...], sh_ref[...]))
        hhi_ref[...] = hhi
        hlo_ref[...] = hlo

    @pl.when(j < 2)
    def _():
        hhi = hhi_ref[...]
        whi = whi_ref[...]
        z = _dot(hhi, whi) + (_dot(hhi, wlo_ref[...]) + _dot(hlo_ref[...], whi))
        zhi, zlo = _split_bf16(z * z)
        ssq = _dot(zhi, bd_ref[...]) + _dot(zlo, bd_ref[...])
        o_ref[...] = z * lax.rsqrt(ssq * (1.0 / A_HEAD_DIM) + EPS) * gqk_ref[...]

    @pl.when(j >= 2)
    def _():
        o_ref[...] = _dot(hhi_ref[...], whi_ref[...])


def _inproj(x2d, sc, sh, g, whi, wlo, gqk, bd, tm):
    n, d = x2d.shape
    groups, r, _ = sc.shape
    tiles_per_group = n // tm // groups
    mod_spec = pl.BlockSpec((None, r, d), lambda i, j: (i // tiles_per_group, 0, 0))
    return pl.pallas_call(
        _inproj_kernel,
        grid=(n // tm, N_COL_BLOCKS),
        in_specs=[pl.BlockSpec((tm, d), lambda i, j: (i, 0)),
                  mod_spec, mod_spec,
                  pl.BlockSpec((1, d), lambda i, j: (0, 0)),
                  pl.BlockSpec((d, COL_BLOCK), lambda i, j: (0, j)),
                  pl.BlockSpec((d, COL_BLOCK), lambda i, j: (0, jnp.minimum(j, 1))),
                  pl.BlockSpec((None, 1, COL_BLOCK), lambda i, j: (jnp.minimum(j, 1), 0, 0)),
                  pl.BlockSpec((COL_BLOCK, COL_BLOCK), lambda i, j: (0, 0))],
        out_specs=pl.BlockSpec((tm, COL_BLOCK), lambda i, j: (i, j)),
        out_shape=jax.ShapeDtypeStruct((n, IN_WIDTH), F32),
        scratch_shapes=[pltpu.VMEM((tm, d), BF16), pltpu.VMEM((tm, d), BF16)],
        compiler_params=_params(("arbitrary", "arbitrary")),
        name="in_proj",
    )(x2d, sc, sh, g, whi, wlo, gqk, bd)


def _third_largest(s):
    m = jnp.max(s, axis=0, keepdims=True)
    for _ in range(MOBA_TOPK - 1):
        s = jnp.where(s >= m, NEG, s)
        m = jnp.max(s, axis=0, keepdims=True)
    return m


MOBA_HEADS_PER_STEP = 4
MOBA_GROUP_WIDTH = MOBA_HEADS_PER_STEP * A_HEAD_DIM


def _moba_prompt_kernel(q_ref, k_ref, vt_ref, bias_ref, o_ref, km_ref, selb_ref, s_ref, acc_ref):
    qi = pl.program_id(2)
    nb, tk, _ = k_ref.shape
    tq = q_ref.shape[0]
    nh = MOBA_HEADS_PER_STEP

    @pl.when(qi == 0)
    def _():
        km_ref[...] = jnp.mean(k_ref[...], axis=1)

    q = q_ref[...]
    lane = lax.broadcasted_iota(jnp.int32, (1, MOBA_GROUP_WIDTH), 1)
    head_lanes = [(lane // A_HEAD_DIM) == h for h in range(nh)]
    blk = lax.broadcasted_iota(jnp.int32, (nb, tq), 0)
    for h in range(nh):
        s = _dot_nt(jnp.where(head_lanes[h], km_ref[...], 0.0), q, HIGHEST)
        s = jnp.where(blk < qi, s, NEG)
        sel = (blk < qi) & (s >= _third_largest(s))
        selb_ref[h] = jnp.where(sel | (blk == qi), 0.0, NEG)

    qs = (q * A_SCALE).astype(BF16)

    n_blocks = qi + 1

    def by_pairs(step, init):
        c = lax.fori_loop(0, n_blocks // 2, lambda i, c: step(2 * i + 1, step(2 * i, c)), init)
        return lax.cond(n_blocks % 2 == 1, lambda c: step(n_blocks - 1, c), lambda c: c, c)

    def logits_of(j, m):
        kb = k_ref[j]
        dslot = jnp.minimum(qi - j, 2)
        out = []
        for h in range(nh):
            kh = jnp.where(head_lanes[h], kb, 0.0).astype(BF16)
            lg = _dot_nt(kh, qs) + bias_ref[h, dslot] + selb_ref[h, pl.ds(j, 1), :]
            s_ref[j, h] = lg
            out.append(jnp.maximum(m[h], jnp.max(lg, axis=0, keepdims=True)))
        return tuple(out)

    m = by_pairs(logits_of, tuple(jnp.full((1, tq), NEG, F32) for _ in range(nh)))

    acc_ref[...] = jnp.zeros_like(acc_ref)

    def weigh(j, l):
        vtb = vt_ref[j].astype(BF16)
        out = []
        for h in range(nh):
            p = jnp.exp(s_ref[j, h] - m[h])
            acc_ref[h] += _dot(vtb[h * A_HEAD_DIM:(h + 1) * A_HEAD_DIM, :], p.astype(BF16))
            out.append(l[h] + jnp.sum(p, axis=0, keepdims=True))
        return tuple(out)

    l = by_pairs(weigh, tuple(jnp.zeros((1, tq), F32) for _ in range(nh)))
    ot = jnp.concatenate([acc_ref[h] / l[h] for h in range(nh)], axis=0)
    o_ref[...] = ot.T


def _moba_prompt(z4, vt, bias_t):
    b, nb, tk, _ = z4.shape
    groups = A_HEADS // MOBA_HEADS_PER_STEP
    w = MOBA_GROUP_WIDTH
    return pl.pallas_call(
        _moba_prompt_kernel,
        grid=(b, groups, nb),
        in_specs=[pl.BlockSpec((None, None, tk, w), lambda bi, g, qi: (bi, qi, 0, g)),
                  pl.BlockSpec((None, nb, tk, w), lambda bi, g, qi: (bi, 0, 0, groups + g)),
                  pl.BlockSpec((None, None, nb, w, tk), lambda bi, g, qi: (bi, g, 0, 0, 0)),
                  pl.BlockSpec((MOBA_HEADS_PER_STEP, 3, tk, tk), lambda bi, g, qi: (g, 0, 0, 0))],
        out_specs=pl.BlockSpec((None, None, tk, w), lambda bi, g, qi: (bi, qi, 0, g)),
        out_shape=jax.ShapeDtypeStruct((b, nb, tk, A_WIDTH), F32),
        scratch_shapes=[pltpu.VMEM((nb, w), F32), pltpu.VMEM((MOBA_HEADS_PER_STEP, nb, tk), F32),
                        pltpu.VMEM((nb, MOBA_HEADS_PER_STEP, tk, tk), F32),
                        pltpu.VMEM((MOBA_HEADS_PER_STEP, A_HEAD_DIM, tk), F32)],
        compiler_params=_params(("arbitrary", "arbitrary", "arbitrary")),
        name="moba_prompt",
    )(z4, z4, vt, bias_t)


def _hgrn_gates(qr, fr, lb_param):
    e = jnp.exp(lb_param - jnp.max(lb_param, axis=0, keepdims=True))
    lb = e[0:1, :] / jnp.sum(e, axis=0, keepdims=True)
    f = lb + (1.0 - lb) * jax.nn.sigmoid(fr)
    return _silu(qr), f


def _hgrn_prompt_kernel(qr_ref, fr_ref, ir_ref, lb_ref, o_ref, s_ref, st_ref):
    tb = pl.program_id(1)
    t = qr_ref.shape[0]
    c = R_CHUNK
    row = lax.broadcasted_iota(jnp.int32, (c, c), 0)
    col = lax.broadcasted_iota(jnp.int32, (c, c), 1)
    tril = row >= col
    tril_f = tril.astype(F32)
    lb = lb_ref[...]

    @pl.when(tb == 0)
    def _():
        st_ref[...] = jnp.zeros_like(st_ref)

    def chunk(ci, _):
        rows = pl.ds(pl.multiple_of(ci * c, c), c)
        q, f = _hgrn_gates(qr_ref[rows, :], fr_ref[rows, :], lb)
        k = 1.0 - f
        v = ir_ref[rows, :]
        b = _dot(tril_f, jnp.log(f), HIGHEST)
        bl = b[c - 1:c, :]
        qe = (q * jnp.exp(b)).astype(BF16)
        ke = (k * jnp.exp(-b)).astype(BF16)
        kd = (k * jnp.exp(bl - b)).astype(BF16)
        decay = jnp.exp(bl)
        outs = []
        for h in range(R_HEADS):
            sl = slice(h * R_KEY_DIM, (h + 1) * R_KEY_DIM)
            st = st_ref[h]
            vh = v[:, sl]
            att = jnp.where(tril, _dot_nt(qe[:, sl], ke[:, sl]), 0.0)
            outs.append(_dot_nt(qe[:, sl], st.astype(BF16)) + _dot(att.astype(BF16), vh.astype(BF16)))
            st_ref[h] = st * decay[:, sl] + _dot(vh.T.astype(BF16), kd[:, sl])
        o_ref[rows, :] = jnp.concatenate(outs, axis=-1)
        return 0

    lax.fori_loop(0, t // c, chunk, 0, unroll=2)

    @pl.when(tb == pl.num_programs(1) - 1)
    def _():
        for h in range(R_HEADS):
            s_ref[h] = st_ref[h].T


def _hgrn_prompt(z2d, lb_param, batch):
    n = z2d.shape[0]
    t = n // batch
    tb = _row_tile(t, 1024)
    col0 = 3 * A_WIDTH // R_WIDTH
    spec = lambda off: pl.BlockSpec((tb, R_WIDTH), lambda bi, ti: (bi * (t // tb) + ti, col0 + off))
    state = pl.BlockSpec((None, R_HEADS, R_KEY_DIM, R_VAL_DIM), lambda bi, ti: (bi, 0, 0, 0))
    return pl.pallas_call(
        _hgrn_prompt_kernel,
        grid=(batch, t // tb),
        in_specs=[spec(0), spec(1), spec(2), pl.BlockSpec(lb_param.shape, lambda bi, ti: (0, 0))],
        out_specs=[pl.BlockSpec((tb, R_WIDTH), lambda bi, ti: (bi * (t // tb) + ti, 0)), state],
        out_shape=[jax.ShapeDtypeStruct((n, R_WIDTH), F32),
                   jax.ShapeDtypeStruct((batch, R_HEADS, R_KEY_DIM, R_VAL_DIM), F32)],
        scratch_shapes=[pltpu.VMEM((R_HEADS, R_VAL_DIM, R_KEY_DIM), F32)],
        compiler_params=_params(("arbitrary", "arbitrary")),
        name="hgrn_prompt",
    )(z2d, z2d, z2d, lb_param)


def _merge_kernel(oa_ref, or_ref, gr_ref, ga0_ref, ga1_ref, gb0_ref, gb1_ref, x_ref, gt_ref, go_ref,
                  wa_ref, wr_ref, wo_ref, o_ref):
    o_r = or_ref[...]
    go = go_ref[...]
    heads = []
    for h in range(R_HEADS):
        v = o_r[:, h * R_VAL_DIM:(h + 1) * R_VAL_DIM]
        heads.append(v * lax.rsqrt(jnp.mean(v * v, axis=-1, keepdims=True) + EPS) * go)
    o_r = jnp.concatenate(heads, axis=-1) * _silu(gr_ref[...])
    ga = jnp.concatenate([ga0_ref[...], ga1_ref[...]], axis=-1)
    gb = jnp.concatenate([gb0_ref[...], gb1_ref[...]], axis=-1)
    merged = (jax.nn.sigmoid(ga) * _dot(oa_ref[...].astype(BF16), wa_ref[...])
              + jax.nn.sigmoid(gb) * _dot(o_r.astype(BF16), wr_ref[...]))
    o_ref[...] = x_ref[...] + gt_ref[...] * _dot(merged.astype(BF16), wo_ref[...])


def _merge(o_a, o_r, z2d, x2d, gt, g_o, wa, wr, wo, tm):
    n, d = x2d.shape
    groups, r, _ = gt.shape
    tiles_per_group = n // tm // groups
    zcol = lambda cb: pl.BlockSpec((tm, COL_BLOCK), lambda i: (i, cb))
    full = lambda a: pl.BlockSpec(a.shape, lambda i: (0,) * a.ndim)
    return pl.pallas_call(
        _merge_kernel,
        grid=(n // tm,),
        in_specs=[pl.BlockSpec((tm, A_WIDTH), lambda i: (i, 0)),
                  pl.BlockSpec((tm, R_WIDTH), lambda i: (i, 0)),
                  zcol(6), zcol(7), zcol(8), zcol(9), zcol(10),
                  pl.BlockSpec((tm, d), lambda i: (i, 0)),
                  pl.BlockSpec((None, r, d), lambda i: (i // tiles_per_group, 0, 0)),
                  full(g_o), full(wa), full(wr), full(wo)],
        out_specs=pl.BlockSpec((tm, d), lambda i: (i, 0)),
        out_shape=jax.ShapeDtypeStruct((n, d), F32),
        compiler_params=_params(("arbitrary",)),
        name="merge_out_proj",
    )(o_a, o_r, z2d, z2d, z2d, z2d, z2d, x2d, gt, g_o, wa, wr, wo)


_STAIR = [P_TOPK // (a + 1) for a in range(P_TOPK)]
LANES = 128
ROWS_PER_BLOCK = 4
E_BLOCK = ROWS_PER_BLOCK * P_NKEYS


def _top_rows(s, n):
    rows = []
    for r in range(n):
        m = jnp.max(s, axis=0, keepdims=True)
        rows.append(m)
        if r + 1 < n:
            s = jnp.where(s >= m, NEG, s)
    return rows


def _stack_rows(rows):
    n = len(rows)
    idx = lax.broadcasted_iota(jnp.int32, (n, LANES), 0)
    out = jnp.zeros((n, LANES), F32)
    for r, row in enumerate(rows):
        out = jnp.where(idx == r, row, out)
    return out


def _top_rows_ranked(s, n):
    rows = []
    rank = jnp.full(s.shape, float(n), F32)
    for r in range(n):
        m = jnp.max(s, axis=0, keepdims=True)
        rows.append(m)
        hit = s >= m
        rank = jnp.where(hit, float(r), rank)
        s = jnp.where(hit, NEG, s)
    return rows, rank


def _route(s0, s1):
    row8 = lax.broadcasted_iota(jnp.int32, (8, LANES), 0)
    v0 = _top_rows(s0, P_TOPK)
    v1, rank1 = _top_rows_ranked(s1, P_TOPK)
    v1_all = _stack_rows(v1)
    cands = [v0[0] + v1_all, v0[1] + v1_all[:8]]
    cands += [jnp.where(row8 < _STAIR[a], v0[a] + v1_all[:8], NEG) for a in range(2, P_TOPK)]
    cand = jnp.concatenate(cands, axis=0)
    tau = _top_rows(cand, P_TOPK)[-1]
    top = v0[0] + v1[0]
    z = jnp.sum(jnp.where(cand >= tau, jnp.exp(cand - top), 0.0), axis=0, keepdims=True)
    n = jnp.zeros(s0.shape, F32)
    for a in range(P_TOPK):
        n_a = jnp.sum(jnp.where(cands[a] >= tau, 1.0, 0.0), axis=0, keepdims=True)
        n = jnp.where(s0 == v0[a], n_a, n)
    return n, jnp.exp(s0 - v0[0]) / z, rank1.astype(BF16), jnp.exp(s1 - v1[0]).astype(BF16)


def _peer_kernel(x_ref, sc_ref, sh_ref, gt_ref, g_ref, wpq_ref, skh_ref, skl_ref, u0_ref, ua_ref, ub_ref,
                 vta_ref, vtb_ref, o_ref, h2_ref, n_ref, e0_ref, r1_ref, e1_ref, a0_ref, a1_ref, g0_ref, g1_ref,
                 acc_ref):
    s = pl.program_id(1)
    tm = x_ref.shape[0]
    nc = tm // LANES
    group = min(nc, 2)

    @pl.when(s == 0)
    def _():
        h2 = _rms_mod(x_ref[...], g_ref[...], sc_ref[...], sh_ref[...]).astype(BF16)
        h2_ref[...] = h2
        q = _dot(h2, wpq_ref[...])
        for h in range(P_HEADS):
            qh, ql = _split_bf16(q[:, h * P_KEY_DIM:(h + 1) * P_KEY_DIM])
            skh = skh_ref[h]
            st = _dot_nt(skh, qh) + (_dot_nt(skh, ql) + _dot_nt(skl_ref[h], qh))
            for c in range(nc):
                n_ref[h * nc + c] = st[:P_NKEYS, c * LANES:(c + 1) * LANES]
                e0_ref[h * nc + c] = st[P_NKEYS:, c * LANES:(c + 1) * LANES]

        def route(pair, _):
            for idx in (2 * pair, 2 * pair + 1):
                n, e0, r1, e1 = _route(n_ref[idx], e0_ref[idx])
                n_ref[idx] = n
                e0_ref[idx] = 0.5 * e0
                r1_ref[idx] = r1.reshape(P_NKEYS // 16, 16, LANES)
                e1_ref[idx] = e1.reshape(P_NKEYS // 16, 16, LANES)
            return 0

        lax.fori_loop(0, P_HEADS * nc // 2, route, 0)
        acc_ref[...] = jnp.zeros_like(acc_ref)
        a0_ref[...] = _dot_nt(u0_ref[...], h2)

    def gate(blk, a_ref, g_ref_, c):
        w = [jnp.zeros((P_NKEYS // 16, 16, LANES), BF16) for _ in range(ROWS_PER_BLOCK)]
        for h in range(P_HEADS):
            idx = h * nc + c
            r1 = r1_ref[idx]
            e1 = e1_ref[idx]
            for ii in range(ROWS_PER_BLOCK):
                i = ROWS_PER_BLOCK * blk + ii
                n_row = jnp.broadcast_to(n_ref[idx, pl.ds(i, 1), :], (16, LANES)).astype(BF16)
                e0_row = jnp.broadcast_to(e0_ref[idx, pl.ds(i, 1), :], (16, LANES)).astype(BF16)
                w[ii] = w[ii] + jnp.where(r1 < n_row[None], e1, jnp.zeros((), BF16)) * e0_row[None]
        for ii in range(ROWS_PER_BLOCK):
            a = a_ref[ii * P_NKEYS:(ii + 1) * P_NKEYS, c * LANES:(c + 1) * LANES]
            act = (a * (1.0 + lax.erf(a * (1.0 / math.sqrt(2.0))))).astype(BF16)
            g_ref_[ii * P_NKEYS:(ii + 1) * P_NKEYS, c * LANES:(c + 1) * LANES] = act * w[ii].reshape(P_NKEYS, LANES)

    def half_step(blk, a_src, a_dst, u_next, g_dst, vt_cur):
        for cg in range(nc // group):
            cols = slice(cg * group * LANES, (cg + 1) * group * LANES)
            a_dst[:, cols] = _dot_nt(u_next[...], h2_ref[cols, :])
            for c in range(cg * group, (cg + 1) * group):
                gate(blk, a_src, g_dst, c)
            acc_ref[:, cols] += _dot(vt_cur[...], g_dst[:, cols])

    half_step(2 * s, a0_ref, a1_ref, ua_ref, g0_ref, vta_ref)
    half_step(2 * s + 1, a1_ref, a0_ref, ub_ref, g1_ref, vtb_ref)

    @pl.when(s == pl.num_programs(1) - 1)
    def _():
        o_ref[...] = x_ref[...] + gt_ref[...] * acc_ref[...].T


def _peer(x2d, sc, sh, gt, g, wpq, skh, skl, u, vt, tm):
    n, d = x2d.shape
    groups, r, _ = sc.shape
    tiles_per_group = n // tm // groups
    nc = tm // LANES
    n_blk = u.shape[0] // E_BLOCK
    mod_spec = pl.BlockSpec((None, r, d), lambda i, s: (i // tiles_per_group, 0, 0))
    route32 = pltpu.VMEM((P_HEADS * nc, P_NKEYS, LANES), F32)
    route16 = pltpu.VMEM((P_HEADS * nc, P_NKEYS // 16, 16, LANES), BF16)
    pre = pltpu.VMEM((E_BLOCK, tm), F32)
    gated = pltpu.VMEM((E_BLOCK, tm), BF16)
    single = pl.Buffered(1)
    return pl.pallas_call(
        _peer_kernel,
        grid=(n // tm, n_blk // 2),
        in_specs=[pl.BlockSpec((tm, d), lambda i, s: (i, 0), pipeline_mode=single),
                  mod_spec, mod_spec, mod_spec,
                  pl.BlockSpec((1, d), lambda i, s: (0, 0)),
                  pl.BlockSpec(wpq.shape, lambda i, s: (0, 0), pipeline_mode=single),
                  pl.BlockSpec(skh.shape, lambda i, s: (0, 0, 0), pipeline_mode=single),
                  pl.BlockSpec(skl.shape, lambda i, s: (0, 0, 0), pipeline_mode=single),
                  pl.BlockSpec((E_BLOCK, d), lambda i, s: (0, 0), pipeline_mode=single),
                  pl.BlockSpec((E_BLOCK, d), lambda i, s: (2 * s + 1, 0)),
                  pl.BlockSpec((E_BLOCK, d), lambda i, s: (jnp.minimum(2 * s + 2, n_blk - 1), 0)),
                  pl.BlockSpec((None, d, E_BLOCK), lambda i, s: (2 * s, 0, 0)),
                  pl.BlockSpec((None, d, E_BLOCK), lambda i, s: (2 * s + 1, 0, 0))],
        out_specs=pl.BlockSpec((tm, d), lambda i, s: (i, 0), pipeline_mode=single),
        out_shape=jax.ShapeDtypeStruct((n, d), F32),
        scratch_shapes=[pltpu.VMEM((tm, d), BF16), route32, route32, route16, route16, pre, pre, gated, gated,
                        pltpu.VMEM((d, tm), F32)],
        compiler_params=_params(("arbitrary", "arbitrary")),
        name="peer",
    )(x2d, sc, sh, gt, g, wpq, skh, skl, u, u, u, vt, vt)


PAGES_PER_STEP = 16
PAGES_PER_BLOCK = MOBA_BLOCK // PAGE_SIZE
TILES = MOBA_TOPK * PAGES_PER_BLOCK
KEYS = TILES * PAGE_SIZE


def _cache_select_kernel(per_step, pt_ref, q_ref, *refs):
    pages, o_ref, s_ref = refs[:per_step], refs[per_step], refs[per_step + 1]
    g = pl.program_id(1)
    blocks_per_step = per_step // PAGES_PER_BLOCK
    nb = s_ref.shape[0]
    lane8 = lax.broadcasted_iota(jnp.int32, (8, LANES), 1)
    row8 = lax.broadcasted_iota(jnp.int32, (8, LANES), 0)

    @pl.when(g == 0)
    def _():
        s_ref[...] = jnp.full(s_ref.shape, NEG, F32)

    rows = jnp.zeros((blocks_per_step, LANES), F32)
    for blk in range(blocks_per_step):
        part = jnp.zeros((8, LANES), F32)
        for h in range(A_HEADS):
            prod = pages[blk * PAGES_PER_BLOCK][h] * q_ref[h]
            for p in range(1, PAGES_PER_BLOCK):
                prod = prod + pages[blk * PAGES_PER_BLOCK + p][h] * q_ref[h]
            folded = prod.reshape(A_HEAD_DIM // 8, 8, PAGE_SIZE).sum(axis=0)
            part = jnp.where(lane8 == h, jnp.sum(folded, axis=-1, keepdims=True), part)
        score = jnp.sum(part, axis=0, keepdims=True) * (1.0 / MOBA_BLOCK)
        rows = jnp.where(row8[:blocks_per_step] == blk, score, rows)
    s_ref[pl.ds(pl.multiple_of(g * blocks_per_step, blocks_per_step), blocks_per_step), :] = rows

    @pl.when(g == pl.num_programs(1) - 1)
    def _():
        sc = s_ref[...]
        row = lax.broadcasted_iota(jnp.int32, sc.shape, 0)
        out_row = lax.broadcasted_iota(jnp.int32, o_ref.shape, 0)
        out = jnp.zeros(o_ref.shape, jnp.int32)
        for r in range(MOBA_TOPK):
            m = jnp.max(sc, axis=0, keepdims=True)
            idx = jnp.min(jnp.where(sc >= m, row, nb), axis=0, keepdims=True)
            out = jnp.where(out_row == r, idx, out)
            sc = jnp.where(row == idx, NEG, sc)
        o_ref[...] = out


def _cache_select(q_cols, cache_t, pt_flat, n_pages):
    batch = q_cols.shape[0]
    _, nh, dh, _ = cache_t.shape
    per_step = math.gcd(n_pages, PAGES_PER_STEP)
    steps = n_pages // per_step
    page = lambda p: pl.BlockSpec((None, nh, dh, PAGE_SIZE),
                                  lambda b, g, pt: (pt[b * n_pages + g * per_step + p], 0, 0, 0))
    return pl.pallas_call(
        functools.partial(_cache_select_kernel, per_step),
        grid_spec=pltpu.PrefetchScalarGridSpec(
            num_scalar_prefetch=1,
            grid=(batch, steps),
            in_specs=[pl.BlockSpec((None, nh, dh, 1), lambda b, g, pt: (b, 0, 0, 0))]
                     + [page(p) for p in range(per_step)],
            out_specs=pl.BlockSpec((None, 8, LANES), lambda b, g, pt: (b, 0, 0)),
            scratch_shapes=[pltpu.VMEM((n_pages // PAGES_PER_BLOCK, LANES), F32)]),
        out_shape=jax.ShapeDtypeStruct((batch, 8, LANES), jnp.int32),
        compiler_params=_params(("arbitrary", "arbitrary")),
        name="cache_select_blocks",
    )(pt_flat, q_cols, *([cache_t] * per_step))


def _moba_sample_kernel(past_len, n_pages, sel_ref, pt_ref, rb_ref, q_ref, kn_ref, vn_ref, ck_ref, cv_ref, o_ref,
                        kbuf, vbuf, sem):
    b = pl.program_id(0)
    nb = pl.num_programs(0)

    def copies(bb, slot):
        out = []
        for h in range(A_HEADS):
            for r in range(MOBA_TOPK):
                blk = sel_ref[(bb * A_HEADS + h) * MOBA_TOPK + r]
                for p in range(PAGES_PER_BLOCK):
                    page = pt_ref[bb * n_pages + blk * PAGES_PER_BLOCK + p]
                    cols = pl.ds((r * PAGES_PER_BLOCK + p) * PAGE_SIZE, PAGE_SIZE)
                    out.append(pltpu.make_async_copy(ck_ref.at[page, h], kbuf.at[slot, h, :, cols], sem.at[slot]))
                    out.append(pltpu.make_async_copy(cv_ref.at[page, h], vbuf.at[slot, h, :, cols], sem.at[slot]))
        return out

    @pl.when(b == 0)
    def _():
        for c in copies(0, 0):
            c.start()

    @pl.when(b + 1 < nb)
    def _():
        for c in copies(b + 1, (b + 1) % 2):
            c.start()

    slot = b % 2
    for c in copies(b, slot):
        c.wait()

    row8 = lax.broadcasted_iota(jnp.int32, (8, A_HEAD_DIM), 0)
    row8k = lax.broadcasted_iota(jnp.int32, (8, KEYS), 0)
    lane = lax.broadcasted_iota(jnp.int32, (1, KEYS), 1)
    for h in range(A_HEADS):
        qh = q_ref[h:h + 1, :] * A_SCALE
        kn = kn_ref[h:h + 1, :]
        vn = vn_ref[h:h + 1, :]
        q8 = jnp.where(row8 == 0, qh, 0.0).astype(BF16)
        start = [past_len - sel_ref[(b * A_HEADS + h) * MOBA_TOPK + r] * MOBA_BLOCK for r in range(MOBA_TOPK)]
        which = lane // MOBA_BLOCK
        dist = jnp.where(which == 0, start[0], jnp.where(which == 1, start[1], start[2])) - lane % MOBA_BLOCK
        logits = _dot(q8, kbuf[slot, h].astype(BF16))[0:1, :] + _bias_of_distance(dist, rb_ref, h)
        own = jnp.sum(qh * kn, axis=-1, keepdims=True) + rb_ref[0, h]
        m = jnp.maximum(own, jnp.max(logits, axis=-1, keepdims=True))
        p_own = jnp.exp(own - m)
        p = jnp.exp(logits - m)
        l = p_own + jnp.sum(p, axis=-1, keepdims=True)
        p8 = jnp.where(row8k == 0, p, 0.0).astype(BF16)
        o_ref[h:h + 1, :] = (_dot_nt(p8, vbuf[slot, h].astype(BF16))[0:1, :] + p_own * vn) / l


def _moba_sample(q3, kn3, vn3, cache_kt, cache_vt, sel_flat, pt_flat, rel_bias, n_pages):
    batch = q3.shape[0]
    row = pl.BlockSpec((None, A_HEADS, A_HEAD_DIM), lambda b, sel, pt: (b, 0, 0))
    buf = pltpu.VMEM((2, A_HEADS, A_HEAD_DIM, KEYS), F32)
    return pl.pallas_call(
        functools.partial(_moba_sample_kernel, n_pages * PAGE_SIZE, n_pages),
        grid_spec=pltpu.PrefetchScalarGridSpec(
            num_scalar_prefetch=2,
            grid=(batch,),
            in_specs=[pl.BlockSpec(memory_space=pltpu.SMEM), row, row, row,
                      pl.BlockSpec(memory_space=pl.ANY), pl.BlockSpec(memory_space=pl.ANY)],
            out_specs=row,
            scratch_shapes=[buf, buf, pltpu.SemaphoreType.DMA((2,))]),
        out_shape=jax.ShapeDtypeStruct((batch, A_HEADS, A_HEAD_DIM), F32),
        compiler_params=_params(("arbitrary",)),
        name="moba_sample",
    )(sel_flat, pt_flat, rel_bias, q3, kn3, vn3, cache_kt, cache_vt)


def _hgrn_sample_kernel(qr_ref, fr_ref, ir_ref, lb_ref, s_ref, o_ref, sn_ref):
    row8 = lax.broadcasted_iota(jnp.int32, (8, LANES), 0)
    eye = (lax.broadcasted_iota(jnp.int32, (LANES, LANES), 0)
           == lax.broadcasted_iota(jnp.int32, (LANES, LANES), 1)).astype(F32)
    outs = []
    for h in range(R_HEADS):
        sl = slice(h * R_KEY_DIM, (h + 1) * R_KEY_DIM)
        q, f = _hgrn_gates(qr_ref[:, sl], fr_ref[:, sl], lb_ref[:, sl])
        k = 1.0 - f
        v = ir_ref[:, sl]
        rows = jnp.where(row8 == 0, q * f, jnp.where(row8 == 1, f, jnp.where(row8 == 2, k, 0.0)))
        cols = _dot_nt(eye, rows, HIGHEST)
        s = s_ref[h]
        outs.append(jnp.sum(s * cols[:, 0:1], axis=0, keepdims=True) + jnp.sum(q * k, axis=-1, keepdims=True) * v)
        sn_ref[h] = s * cols[:, 1:2] + cols[:, 2:3] * v
    o_ref[...] = jnp.concatenate(outs, axis=-1)


def _hgrn_sample(z3, lb_param, state):
    batch = z3.shape[0]
    col = lambda cb: pl.BlockSpec((None, 1, R_WIDTH), lambda b: (b, 0, cb))
    return pl.pallas_call(
        _hgrn_sample_kernel,
        grid=(batch,),
        in_specs=[col(3), col(4), col(5), pl.BlockSpec(lb_param.shape, lambda b: (0, 0)),
                  pl.BlockSpec((None, R_HEADS, R_KEY_DIM, R_VAL_DIM), lambda b: (b, 0, 0, 0))],
        out_specs=[pl.BlockSpec((None, 1, R_WIDTH), lambda b: (b, 0, 0)),
                   pl.BlockSpec((None, R_HEADS, R_KEY_DIM, R_VAL_DIM), lambda b: (b, 0, 0, 0))],
        out_shape=[jax.ShapeDtypeStruct((batch, 1, R_WIDTH), F32),
                   jax.ShapeDtypeStruct(state.shape, F32)],
        compiler_params=_params(("arbitrary",)),
        name="hgrn_sample",
    )(z3, z3, z3, lb_param, state)


def _row_tile(n, preferred):
    t = min(n, preferred)
    while n % t:
        t //= 2
    return t


def kernel(x_prompt, x_sample, c_prompt, c_sample, cache_k, cache_v, state_hgrn, page_table, rel_bias, lb_param, w_ada, b_ada, g_norm1, w_in, g_qnorm, g_knorm, w_a_up, w_r_up, g_onorm, w_out, g_norm2, w_pq, sub_keys, expert_u, expert_v):
    assert w_ada.shape[0] == 1, "single-layer trunk"
    bp, t, d = x_prompt.shape
    bs, ts, _ = x_sample.shape
    n_pages = page_table.shape[1]
    assert ts == 1 and t % MOBA_BLOCK == 0 and n_pages % PAGES_PER_BLOCK == 0
    assert n_pages // PAGES_PER_BLOCK >= MOBA_TOPK

    w_hi, w_lo = _split_bf16(w_in[0])
    w_lo = w_lo[:, :2 * COL_BLOCK]
    gqk = jnp.stack([jnp.tile(g_qnorm[0], A_HEADS), jnp.tile(g_knorm[0], A_HEADS)]).reshape(2, 1, A_WIDTH)
    head_of = np.arange(A_WIDTH) // A_HEAD_DIM
    bd = jnp.asarray(head_of[:, None] == head_of[None, :], BF16)
    wa, wr, wo, wpq = (w[0].astype(BF16) for w in (w_a_up, w_r_up, w_out, w_pq))
    sk2 = jnp.zeros((P_HEADS, 2 * P_NKEYS, P_KEY_DIM), F32)
    sk2 = sk2.at[:, :P_NKEYS, :P_HALF].set(sub_keys[0, :, 0]).at[:, P_NKEYS:, P_HALF:].set(sub_keys[0, :, 1])
    sk_hi, sk_lo = _split_bf16(sk2)
    u = expert_u[0].astype(BF16)
    vt = expert_v[0].astype(BF16).reshape(-1, E_BLOCK, d).transpose(0, 2, 1)
    g1, g2, go = g_norm1[0].reshape(1, d), g_norm2[0].reshape(1, d), g_onorm[0].reshape(1, R_VAL_DIM)

    n_c = bp + bs
    c_all = jnp.concatenate([c_prompt, c_sample, jnp.zeros((-n_c % 8, d), F32)], axis=0)
    mod = _ada(c_all, w_ada[0], b_ada[0])
    mod_p = mod[:bp].reshape(bp, 6, 1, d)
    mod_s = mod[bp:n_c].reshape(1, bs, 6, d)
    sh1p, sc1p, gt1p, sh2p, sc2p, gt2p = (mod_p[:, i] for i in range(6))
    sh1s, sc1s, gt1s, sh2s, sc2s, gt2s = (mod_s[:, :, i] for i in range(6))
    bias_t = _bias_tiles(rel_bias)

    xp = x_prompt.reshape(bp * t, d)
    zp = _inproj(xp, sc1p, sh1p, g1, w_hi, w_lo, gqk, bd, _row_tile(t, 1024))
    nb = t // MOBA_BLOCK
    vt_p = zp[:, 2 * A_WIDTH:3 * A_WIDTH].reshape(bp, nb, MOBA_BLOCK, A_HEADS // MOBA_HEADS_PER_STEP, MOBA_GROUP_WIDTH)
    vt_p = vt_p.transpose(0, 3, 1, 4, 2)
    oa_p = _moba_prompt(zp.reshape(bp, nb, MOBA_BLOCK, IN_WIDTH), vt_p, bias_t).reshape(bp * t, A_WIDTH)
    or_p, s_prompt = _hgrn_prompt(zp, lb_param, bp)
    x1p = _merge(oa_p, or_p, zp, xp, gt1p, go, wa, wr, wo, _row_tile(t, 512))
    y_prompt = _peer(x1p, sc2p, sh2p, gt2p, g2, wpq, sk_hi, sk_lo, u, vt, _row_tile(t, 1024))

    xs = x_sample.reshape(bs, d)
    zs = _inproj(xs, sc1s, sh1s, g1, w_hi, w_lo, gqk, bd, bs)
    z3 = zs.reshape(bs, 1, IN_WIDTH)
    pt_flat = page_table.reshape(-1)
    ck_t = jnp.transpose(cache_k[0], (0, 2, 3, 1))
    cv_t = jnp.transpose(cache_v[0], (0, 2, 3, 1))
    new3 = lambda lo: zs[:, lo:lo + A_WIDTH].reshape(bs, A_HEADS, A_HEAD_DIM)
    q3 = new3(0)
    sel = _cache_select(q3[..., None], ck_t, pt_flat, n_pages)[:, :MOBA_TOPK, :A_HEADS].transpose(0, 2, 1).reshape(-1)
    oa_s = _moba_sample(q3, new3(A_WIDTH), new3(2 * A_WIDTH), ck_t, cv_t, sel, pt_flat, rel_bias,
                        n_pages).reshape(bs, A_WIDTH)
    or_s, s_sample = _hgrn_sample(z3, lb_param, state_hgrn[0])
    x1s = _merge(oa_s, or_s.reshape(bs, R_WIDTH), zs, xs, gt1s, go, wa, wr, wo, bs)
    y_sample = _peer(x1s, sc2s, sh2s, gt2s, g2, wpq, sk_hi, sk_lo, u, vt, bs)

    heads = lambda z2, lo, lead: z2[:, lo:lo + A_WIDTH].reshape(1, *lead, A_HEADS, A_HEAD_DIM)
    return (y_prompt.reshape(bp, t, d), y_sample.reshape(bs, 1, d),
            heads(zp, A_WIDTH, (bp, t)), heads(zp, 2 * A_WIDTH, (bp, t)), s_prompt[None],
            heads(zs, A_WIDTH, (bs, 1)), heads(zs, 2 * A_WIDTH, (bs, 1)), s_sample[None])
```
